```python
import jax
import jax.numpy as jnp
from jax import lax
import numpy as np

D_MODEL = 1024
BATCH = 2
SEQ = 8192
DEPTH = 4

GRID_W = 64
CTX_LEN = 256
HEAD_DIM = 64
N_Q_HEADS = 12
N_KV_HEADS = 4
GQA_GROUP = N_Q_HEADS // N_KV_HEADS
N_FOURIER_GROUPS = 4
FOURIER_GROUP_DIM = 64
D_Q = N_Q_HEADS * HEAD_DIM
D_KV = N_KV_HEADS * HEAD_DIM
D_FOURIER = N_FOURIER_GROUPS * FOURIER_GROUP_DIM
D_IN_EVEN = D_Q + 2 * D_KV + D_FOURIER
D_CAT_EVEN = D_Q + D_FOURIER
ROPE_PAIRS_PER_AXIS = HEAD_DIM // 4
ROPE_THETA = 10000.0
Q_BLOCK = 128
CONV_WIDTH = 31
D_CONV = D_MODEL
D_FF = 4 * D_MODEL
N_MOD = 6
EPS = 1e-6
ATTN_SCALE = HEAD_DIM ** -0.5

kernel_name = "hybrid_gqa_fourier_conformer_dit_prefix"


def rms_norm(x, g):
    xf = x.astype(jnp.float32)
    y = xf * lax.rsqrt(jnp.mean(xf * xf, axis=-1, keepdims=True) + EPS)
    return (y * g.astype(jnp.float32)).astype(x.dtype)


def layer_norm(x, g, b):
    xf = x.astype(jnp.float32)
    mu = jnp.mean(xf, axis=-1, keepdims=True)
    var = jnp.mean(jnp.square(xf - mu), axis=-1, keepdims=True)
    y = (xf - mu) * lax.rsqrt(var + EPS)
    return (y * g.astype(jnp.float32) + b.astype(jnp.float32)).astype(x.dtype)


def adaln_params(s, w, b):
    m = (s @ w + b).reshape(s.shape[0], 1, N_MOD, D_MODEL)
    return tuple(m[:, :, j] for j in range(N_MOD))


def modulate(h, shift, scale):
    return h * (1 + scale) + shift


def axial_rope_tables(row_idx, col_idx):
    freqs = ROPE_THETA ** (-jnp.arange(ROPE_PAIRS_PER_AXIS, dtype=jnp.float32) / ROPE_PAIRS_PER_AXIS)
    ang = jnp.concatenate([row_idx.astype(jnp.float32)[:, None] * freqs,
                           col_idx.astype(jnp.float32)[:, None] * freqs], axis=-1)
    return jnp.cos(ang), jnp.sin(ang)


def apply_rope(x, cos, sin):
    xp = x.reshape(*x.shape[:-1], HEAD_DIM // 2, 2)
    x1, x2 = xp[..., 0], xp[..., 1]
    cs = cos[None, :, None, :].astype(x.dtype)
    sn = sin[None, :, None, :].astype(x.dtype)
    out = jnp.stack([x1 * cs - x2 * sn, x1 * sn + x2 * cs], axis=-1)
    return out.reshape(x.shape)


def attend(q, k, v):
    b, lq = q.shape[:2]
    qg = q.reshape(b, lq, N_KV_HEADS, GQA_GROUP, HEAD_DIM)
    s = jnp.einsum("bqkgd,bskd->bkgqs", qg, k, preferred_element_type=jnp.float32) * ATTN_SCALE
    p = jax.nn.softmax(s, axis=-1).astype(v.dtype)
    return jnp.einsum("bkgqs,bskd->bqkgd", p, v).reshape(b, lq, D_Q)


def attend_blocks(q, k, v):
    b, s = q.shape[:2]
    nb = s // Q_BLOCK
    qb = q.reshape(b, nb, Q_BLOCK, N_Q_HEADS, HEAD_DIM).swapaxes(0, 1)
    out = lax.map(lambda q_blk: attend(q_blk, k, v), qb)
    return out.swapaxes(0, 1).reshape(b, s, D_Q)


def fourier_mix(f):
    b, l, _ = f.shape
    fg = f.reshape(b, l, N_FOURIER_GROUPS, FOURIER_GROUP_DIM).astype(jnp.float32)
    out = jnp.fft.fftn(fg, axes=(1, 3), norm="ortho").real
    return out.reshape(b, l, D_FOURIER).astype(f.dtype)


def even_mixer(h_x, h_c, w_in, q_g, k_g, w_out, cos, sin, need_ctx):
    def project(h):
        p = h @ w_in
        bsz, l = h.shape[:2]
        q = rms_norm(p[..., :D_Q].reshape(bsz, l, N_Q_HEADS, HEAD_DIM), q_g)
        k = rms_norm(p[..., D_Q:D_Q + D_KV].reshape(bsz, l, N_KV_HEADS, HEAD_DIM), k_g)
        v = p[..., D_Q + D_KV:D_Q + 2 * D_KV].reshape(bsz, l, N_KV_HEADS, HEAD_DIM)
        f = p[..., D_Q + 2 * D_KV:]
        return q, k, v, f

    qx, kx, vx, fx = project(h_x)
    qc, kc, vc, fc = project(h_c)
    qx = apply_rope(qx, cos, sin)
    kx = apply_rope(kx, cos, sin)
    k_all = jnp.concatenate([kx, kc], axis=1)
    v_all = jnp.concatenate([vx, vc], axis=1)
    y_x = jnp.concatenate([attend_blocks(qx, k_all, v_all), fourier_mix(fx)], axis=-1) @ w_out
    y_c = None
    if need_ctx:
        y_c = jnp.concatenate([attend(qc, kc, vc), fourier_mix(fc)], axis=-1) @ w_out
    return y_x, y_c


def conformer_conv(h, w_pw1, b_pw1, w_dw, b_dw, ln_g, ln_b, w_pw2, b_pw2):
    a = h @ w_pw1 + b_pw1
    u = a[..., :D_CONV] * jax.nn.sigmoid(a[..., D_CONV:])
    pad = CONV_WIDTH // 2
    u = lax.conv_general_dilated(u, w_dw[:, None, :].astype(u.dtype), window_strides=(1,),
                                 padding=[(pad, pad)], dimension_numbers=("NWC", "WIO", "NWC"),
                                 feature_group_count=D_CONV) + b_dw
    u = jax.nn.silu(layer_norm(u, ln_g, ln_b))
    return u @ w_pw2 + b_pw2


def sq_relu_mlp(h, w1, w2):
    return jnp.square(jax.nn.relu(h @ w1)) @ w2


def setup_inputs(seed: int = 0) -> dict:
    key = jax.random.key(seed)
    ks = jax.random.split(key, 24)
    n_even = (DEPTH + 1) // 2
    n_odd = DEPTH // 2

    def nrm(k, shape, scale):
        return jax.random.normal(k, shape, jnp.float32) * scale

    def gain(k, shape):
        return 1.0 + nrm(k, shape, 0.02)

    return {
        "x": nrm(ks[0], (BATCH, SEQ, D_MODEL), 1.0),
        "c": nrm(ks[1], (BATCH, D_MODEL), 1.0),
        "ctx": nrm(ks[2], (BATCH, CTX_LEN, D_MODEL), 1.0),
        "c_ctx": nrm(ks[3], (D_MODEL,), 1.0),
        "ada_w": nrm(ks[4], (DEPTH, D_MODEL, N_MOD * D_MODEL), 0.5 * D_MODEL ** -0.5),
        "ada_b": nrm(ks[5], (DEPTH, N_MOD * D_MODEL), 0.02),
        "norm1_g": gain(ks[6], (DEPTH, D_MODEL)),
        "norm2_g": gain(ks[7], (DEPTH, D_MODEL)),
        "mlp_w1": nrm(ks[8], (DEPTH, D_MODEL, D_FF), D_MODEL ** -0.5),
        "mlp_w2": nrm(ks[9], (DEPTH, D_FF, D_MODEL), D_FF ** -0.5),
        "attn_w_in": nrm(ks[10], (n_even, D_MODEL, D_IN_EVEN), D_MODEL ** -0.5),
        "q_norm_g": gain(ks[11], (n_even, HEAD_DIM)),
        "k_norm_g": gain(ks[12], (n_even, HEAD_DIM)),
        "attn_w_out": nrm(ks[13], (n_even, D_CAT_EVEN, D_MODEL), D_CAT_EVEN ** -0.5),
        "conv_w_pw1": nrm(ks[14], (n_odd, D_MODEL, 2 * D_CONV), D_MODEL ** -0.5),
        "conv_b_pw1": nrm(ks[15], (n_odd, 2 * D_CONV), 0.02),
        "conv_w_dw": nrm(ks[16], (n_odd, CONV_WIDTH, D_CONV), CONV_WIDTH ** -0.5),
        "conv_b_dw": nrm(ks[17], (n_odd, D_CONV), 0.02),
        "conv_ln_g": gain(ks[18], (n_odd, D_CONV)),
        "conv_ln_b": nrm(ks[19], (n_odd, D_CONV), 0.02),
        "conv_w_pw2": nrm(ks[20], (n_odd, D_CONV, D_MODEL), D_CONV ** -0.5),
        "conv_b_pw2": nrm(ks[21], (n_odd, D_MODEL), 0.02),
    }


def reference(x, c, ctx, c_ctx, ada_w, ada_b, norm1_g, norm2_g, mlp_w1, mlp_w2,
              attn_w_in, q_norm_g, k_norm_g, attn_w_out,
              conv_w_pw1, conv_b_pw1, conv_w_dw, conv_b_dw, conv_ln_g, conv_ln_b,
              conv_w_pw2, conv_b_pw2):
    seq = x.shape[1]
    rows = seq // GRID_W
    row_idx = jnp.repeat(jnp.arange(rows, dtype=jnp.int32), GRID_W)
    col_idx = jnp.tile(jnp.arange(GRID_W, dtype=jnp.int32), rows)
    cos, sin = axial_rope_tables(row_idx, col_idx)
    s_x = jax.nn.silu(c)
    s_c = jax.nn.silu(c_ctx)[None, :]

    for i in range(DEPTH):
        even = i % 2 == 0
        j = i // 2
        ctx_update = i != DEPTH - 1
        sh1, sc1, g1, sh2, sc2, g2 = adaln_params(s_x, ada_w[i], ada_b[i])
        h_x = modulate(rms_norm(x, norm1_g[i]), sh1, sc1)
        if even or ctx_update:
            csh1, csc1, cg1, csh2, csc2, cg2 = adaln_params(s_c, ada_w[i], ada_b[i])
            h_c = modulate(rms_norm(ctx, norm1_g[i]), csh1, csc1)
        if even:
            y_x, y_c = even_mixer(h_x, h_c, attn_w_in[j], q_norm_g[j], k_norm_g[j], attn_w_out[j],
                                  cos, sin, ctx_update)
        else:
            conv_args = (conv_w_pw1[j], conv_b_pw1[j], conv_w_dw[j], conv_b_dw[j],
                         conv_ln_g[j], conv_ln_b[j], conv_w_pw2[j], conv_b_pw2[j])
            y_x = conformer_conv(h_x, *conv_args)
            y_c = conformer_conv(h_c, *conv_args) if ctx_update else None
        x = x + g1 * y_x
        x = x + g2 * sq_relu_mlp(modulate(rms_norm(x, norm2_g[i]), sh2, sc2), mlp_w1[i], mlp_w2[i])
        if ctx_update:
            ctx = ctx + cg1 * y_c
            ctx = ctx + cg2 * sq_relu_mlp(modulate(rms_norm(ctx, norm2_g[i]), csh2, csc2),
                                          mlp_w1[i], mlp_w2[i])
    return x
```

```python
import functools

import numpy as np
import jax
import jax.numpy as jnp
from jax import lax
from jax.experimental import pallas as pl
from jax.experimental.pallas import tpu as pltpu

D_MODEL = 1024
BATCH = 2
SEQ = 8192
DEPTH = 4
GRID_W = 64
CTX_LEN = 256
HEAD_DIM = 64
N_Q_HEADS = 12
N_KV_HEADS = 4
GQA_GROUP = N_Q_HEADS // N_KV_HEADS
D_Q = N_Q_HEADS * HEAD_DIM
D_KV = N_KV_HEADS * HEAD_DIM
D_FOURIER = 256
FOURIER_GROUP_DIM = 64
ROPE_PAIRS_PER_AXIS = HEAD_DIM // 4
ROPE_THETA = 10000.0
CONV_WIDTH = 31
D_FF = 4 * D_MODEL
N_MOD = 6
EPS = 1e-6
ATTN_SCALE = HEAD_DIM ** -0.5

N_X = BATCH * SEQ
N_C = BATCH * CTX_LEN
N_TOK = N_X + N_C
N_HEADS_QK = N_Q_HEADS + N_KV_HEADS
HALF = HEAD_DIM // 2

LANES = 128
SUBLANES = 8
MIB = 1024 * 1024

TM = 512
X_BLOCKS = N_X // TM
TOK_BLOCKS = N_TOK // TM
BLOCKS_PER_BATCH = SEQ // TM
TQ = CTX_LEN
KB = TM
NQ = SEQ // TQ
N_KCHUNKS = SEQ // KB
FF_CHUNK = 512
TN_ADA = 1536
FFT_A = 128
FFT_B = SEQ // FFT_A
TA = 8
TN_F2 = 2048
CONV_TM = CTX_LEN
CONV_HALO = 16
CONV_RC = 64
PAD = CONV_WIDTH // 2

F32 = jnp.float32
BF16 = jnp.bfloat16
HIGHEST = lax.Precision.HIGHEST


def _rope_tables_t():
    freqs = (np.float32(ROPE_THETA) ** (-np.arange(ROPE_PAIRS_PER_AXIS, dtype=np.float32)
                                        / np.float32(ROPE_PAIRS_PER_AXIS))).astype(np.float32)
    t = np.arange(SEQ)
    row = (t // GRID_W).astype(np.float32)
    col = (t % GRID_W).astype(np.float32)
    ang = np.concatenate([row[:, None] * freqs, col[:, None] * freqs], axis=-1).astype(np.float32)
    cos = np.cos(ang.astype(np.float64)).T
    sin = np.sin(ang.astype(np.float64)).T
    cos = np.concatenate([cos, cos, np.ones((HALF, N_C))], axis=1)
    sin = np.concatenate([sin, sin, np.zeros((HALF, N_C))], axis=1)
    return cos.astype(np.float32), sin.astype(np.float32)


def _dft_cos_sin(n_out, n_in, period):
    k = np.arange(n_out)[:, None].astype(np.int64)
    n = np.arange(n_in)[None, :].astype(np.int64)
    ang = 2.0 * np.pi * ((k * n) % period).astype(np.float64) / period
    return np.cos(ang), np.sin(ang)


def _channel_dft_tables():
    c, s = _dft_cos_sin(FOURIER_GROUP_DIM, FOURIER_GROUP_DIM, FOURIER_GROUP_DIM)
    n_groups = D_FOURIER // FOURIER_GROUP_DIM
    cc = np.kron(np.eye(n_groups), c)
    sc = np.kron(np.eye(n_groups), s)
    return cc.astype(np.float32), sc.astype(np.float32)


def _stage1_tables():
    bp = np.arange(FFT_B)[None, :, None].astype(np.int64)
    a = np.arange(FFT_A)[:, None, None].astype(np.int64)
    b = np.arange(FFT_B)[None, None, :].astype(np.int64)
    ang = 2.0 * np.pi * ((bp * (a + FFT_A * b)) % SEQ).astype(np.float64) / SEQ
    return np.concatenate([np.cos(ang), np.sin(ang)], axis=1).astype(np.float32)


_ROPE_COS_T, _ROPE_SIN_T = _rope_tables_t()
_CC, _SC = _channel_dft_tables()
_STAGE1 = _stage1_tables()
_C128, _S128 = (t.astype(np.float32) for t in _dft_cos_sin(FFT_A, FFT_A, FFT_A))
_C256, _S256 = (t.astype(np.float32) for t in _dft_cos_sin(CTX_LEN, CTX_LEN, CTX_LEN))
FOURIER_SCALE_X = float((SEQ * FOURIER_GROUP_DIM) ** -0.5)
FOURIER_SCALE_C = float((CTX_LEN * FOURIER_GROUP_DIM) ** -0.5)


def _mod_block(i):
    return jnp.minimum(i // BLOCKS_PER_BATCH, BATCH)


def _const_spec(shape):
    nd = len(shape)
    return pl.BlockSpec(shape, lambda *_: (0,) * nd)


def _norm_mod(x, g, shift, scale):
    ms = jnp.mean(x * x, axis=-1, keepdims=True)
    return (x * lax.rsqrt(ms + EPS) * g) * (1.0 + scale) + shift


def _params(vmem_mib):
    return pltpu.CompilerParams(vmem_limit_bytes=vmem_mib * MIB)


def _ada_kernel(c_ref, w_ref, b_ref, o_ref):
    c = c_ref[...]
    s = c * jax.nn.sigmoid(c)
    o_ref[...] = jnp.dot(s.astype(BF16), w_ref[...].astype(BF16),
                         preferred_element_type=F32) + b_ref[...]


def _ada_call(c8, ada_w, ada_b):
    n_out = N_MOD * D_MODEL
    return pl.pallas_call(
        _ada_kernel,
        grid=(DEPTH, n_out // TN_ADA),
        in_specs=[
            _const_spec((SUBLANES, D_MODEL)),
            pl.BlockSpec((None, D_MODEL, TN_ADA), lambda l, j: (l, 0, j)),
            pl.BlockSpec((None, 1, TN_ADA), lambda l, j: (l, 0, j)),
        ],
        out_specs=pl.BlockSpec((None, SUBLANES, TN_ADA), lambda l, j: (l, 0, j)),
        out_shape=jax.ShapeDtypeStruct((DEPTH, SUBLANES, n_out), F32),
        compiler_params=_params(40),
        name="ada",
    )(c8, ada_w, ada_b.reshape(DEPTH, 1, n_out))


def _proj_even_kernel(x_ref, mod_ref, g_ref, wqkvt_ref, wf_ref, gain_ref, cos_ref, sin_ref,
                      cc_ref, sc_ref, qt_ref, k_ref, vt_ref, w_ref, kt_scr):
    h = _norm_mod(x_ref[...], g_ref[...], mod_ref[0:1, :], mod_ref[1:2, :]).astype(BF16)
    t = lax.dot_general(wqkvt_ref[...], h, (((1,), (1,)), ((), ())), preferred_element_type=F32)
    cos = cos_ref[...]
    sin = sin_ref[...]
    for hd in range(N_HEADS_QK):
        lo = HEAD_DIM * hd
        x1 = t[lo:lo + HALF]
        x2 = t[lo + HALF:lo + HEAD_DIM]
        ms = (jnp.sum(x1 * x1, axis=0, keepdims=True)
              + jnp.sum(x2 * x2, axis=0, keepdims=True)) * (1.0 / HEAD_DIM)
        r = lax.rsqrt(ms + EPS)
        x1 = x1 * r * gain_ref[lo:lo + HALF, :]
        x2 = x2 * r * gain_ref[lo + HALF:lo + HEAD_DIM, :]
        o1 = x1 * cos - x2 * sin
        o2 = x1 * sin + x2 * cos
        if hd < N_Q_HEADS:
            qt_ref[lo:lo + HALF, :] = (o1 * ATTN_SCALE).astype(BF16)
            qt_ref[lo + HALF:lo + HEAD_DIM, :] = (o2 * ATTN_SCALE).astype(BF16)
        else:
            kt_scr[lo - D_Q:lo - D_Q + HALF, :] = o1
            kt_scr[lo - D_Q + HALF:lo - D_Q + HEAD_DIM, :] = o2
    k_ref[...] = kt_scr[...].T.astype(BF16)
    vt_ref[...] = t[D_Q + D_KV:D_Q + 2 * D_KV].astype(BF16)
    f = jnp.dot(h, wf_ref[...], preferred_element_type=F32)
    w_ref[:, 0:D_FOURIER] = jnp.dot(f, cc_ref[...], precision=HIGHEST, preferred_element_type=F32)
    w_ref[:, D_FOURIER:2 * D_FOURIER] = -jnp.dot(f, sc_ref[...], precision=HIGHEST,
                                                 preferred_element_type=F32)


def _proj_even_call(tok, mod, g, wqkvt, wf, gain):
    d_qkv = D_Q + 2 * D_KV
    return pl.pallas_call(
        _proj_even_kernel,
        grid=(TOK_BLOCKS,),
        in_specs=[
            pl.BlockSpec((TM, D_MODEL), lambda i: (i, 0)),
            pl.BlockSpec((None, N_MOD, D_MODEL), lambda i: (_mod_block(i), 0, 0)),
            _const_spec((1, D_MODEL)),
            _const_spec((d_qkv, D_MODEL)),
            _const_spec((D_MODEL, D_FOURIER)),
            _const_spec((D_Q + D_KV, 1)),
            pl.BlockSpec((HALF, TM), lambda i: (0, i)),
            pl.BlockSpec((HALF, TM), lambda i: (0, i)),
            _const_spec((D_FOURIER, D_FOURIER)),
            _const_spec((D_FOURIER, D_FOURIER)),
        ],
        out_specs=[
            pl.BlockSpec((D_Q, TM), lambda i: (0, i)),
            pl.BlockSpec((TM, D_KV), lambda i: (i, 0)),
            pl.BlockSpec((None, D_KV, TM), lambda i: (i, 0, 0)),
            pl.BlockSpec((TM, 2 * D_FOURIER), lambda i: (i, 0)),
        ],
        out_shape=[
            jax.ShapeDtypeStruct((D_Q, N_TOK), BF16),
            jax.ShapeDtypeStruct((N_TOK, D_KV), BF16),
            jax.ShapeDtypeStruct((TOK_BLOCKS, D_KV, TM), BF16),
            jax.ShapeDtypeStruct((N_TOK, 2 * D_FOURIER), F32),
        ],
        scratch_shapes=[pltpu.VMEM((D_KV, TM), F32)],
        compiler_params=_params(48),
        name="proj_even",
    )(tok, mod, g, wqkvt, wf, gain, _ROPE_COS_T, _ROPE_SIN_T, _CC, _SC)


def _attn_kernel(qt_ref, kx_ref, kc_ref, vx_ref, vc_ref, o_ref, ot_scr):
    qi = pl.program_id(1)
    n_latent_chunks = jnp.where(qi < NQ, N_KCHUNKS, 0)

    def step(k_chunk, vt_chunk, qpad, carry):
        m, l, acc = carry
        st = jnp.dot(k_chunk, qpad, preferred_element_type=F32)
        m_new = jnp.maximum(m, jnp.max(st, axis=0, keepdims=True))
        alpha = jnp.exp(m - m_new)
        p = jnp.exp(st - m_new)
        l = alpha * l + jnp.sum(p, axis=0, keepdims=True)
        acc = alpha * acc + jnp.dot(vt_chunk, p.astype(BF16), preferred_element_type=F32)
        return m_new, l, acc

    for g in range(N_KV_HEADS):
        pair, side = divmod(g, 2)
        lanes = slice(LANES * pair, LANES * (pair + 1))
        rows = slice(HEAD_DIM * g, HEAD_DIM * (g + 1))

        def head_body(hh, _, g=g, side=side, lanes=lanes, rows=rows):
            row0 = pl.multiple_of((GQA_GROUP * g + hh) * HEAD_DIM, HEAD_DIM)
            q_h = qt_ref[pl.ds(row0, HEAD_DIM), :]
            zero = jnp.zeros_like(q_h)
            qpad = jnp.concatenate([q_h, zero] if side == 0 else [zero, q_h], axis=0)

            def chunk_body(c, carry):
                k0 = pl.multiple_of(c * KB, KB)
                return step(kx_ref[pl.ds(k0, KB), lanes], vx_ref[c, rows, :], qpad, carry)

            init = (jnp.full((1, TQ), -jnp.inf, F32), jnp.zeros((1, TQ), F32),
                    jnp.zeros((HEAD_DIM, TQ), F32))
            carry = lax.fori_loop(0, n_latent_chunks, chunk_body, init)
            _, l, acc = step(kc_ref[:, lanes], vc_ref[rows, :], qpad, carry)
            ot_scr[pl.ds(row0, HEAD_DIM), :] = acc / l
            return 0

        lax.fori_loop(0, GQA_GROUP, head_body, 0)
    o_ref[...] = ot_scr[...].T.astype(BF16)


def _attn_call(qt, k, vt3):
    def q_block(b, qi):
        return jnp.where(qi < NQ, b * NQ + qi, N_X // TQ + b)

    return pl.pallas_call(
        _attn_kernel,
        grid=(BATCH, NQ + 1),
        in_specs=[
            pl.BlockSpec((D_Q, TQ), lambda b, qi: (0, q_block(b, qi))),
            pl.BlockSpec((SEQ, D_KV), lambda b, qi: (b, 0)),
            pl.BlockSpec((CTX_LEN, D_KV), lambda b, qi: (N_X // CTX_LEN + b, 0)),
            pl.BlockSpec((N_KCHUNKS, D_KV, KB), lambda b, qi: (b, 0, 0)),
            pl.BlockSpec((None, D_KV, CTX_LEN), lambda b, qi: (X_BLOCKS, 0, b)),
        ],
        out_specs=pl.BlockSpec((TQ, D_Q), lambda b, qi: (q_block(b, qi), 0)),
        out_shape=jax.ShapeDtypeStruct((N_TOK, D_Q), BF16),
        scratch_shapes=[pltpu.VMEM((D_Q, TQ), F32)],
        compiler_params=_params(48),
        name="attn",
    )(qt, k, k, vt3, vt3)


def _fourier1_kernel(w_ref, m_ref, z_ref):
    for j in range(TA):
        xa = w_ref[:, j, :]
        y = jnp.dot(m_ref[j], xa, precision=HIGHEST, preferred_element_type=F32)
        z_ref[j, 0:FFT_B, :] = y[0:FFT_B, 0:D_FOURIER] + y[FFT_B:, D_FOURIER:]
        z_ref[j, FFT_B:, :] = y[0:FFT_B, D_FOURIER:] - y[FFT_B:, 0:D_FOURIER]


def _fourier1_call(w):
    w3 = w.reshape(N_TOK // FFT_A, FFT_A, 2 * D_FOURIER)
    return pl.pallas_call(
        _fourier1_kernel,
        grid=(BATCH, FFT_A // TA),
        in_specs=[
            pl.BlockSpec((FFT_B, TA, 2 * D_FOURIER), lambda b, a: (b, a, 0)),
            pl.BlockSpec((TA, 2 * FFT_B, FFT_B), lambda b, a: (a, 0, 0)),
        ],
        out_specs=pl.BlockSpec((None, TA, 2 * FFT_B, D_FOURIER), lambda b, a: (b, a, 0, 0)),
        out_shape=jax.ShapeDtypeStruct((BATCH, FFT_A, 2 * FFT_B, D_FOURIER), F32),
        compiler_params=_params(32),
        name="fourier1",
    )(w3, _STAGE1)


def _fourier2_kernel(zr_ref, zi_ref, c_ref, s_ref, o_ref):
    o_ref[...] = (jnp.dot(c_ref[...], zr_ref[...], precision=HIGHEST, preferred_element_type=F32)
                  + jnp.dot(s_ref[...], zi_ref[...], precision=HIGHEST, preferred_element_type=F32)
                  ) * FOURIER_SCALE_X


def _fourier2_call(z):
    half = FFT_B * D_FOURIER
    z2 = z.reshape(BATCH, FFT_A, 2 * half)
    n_j = half // TN_F2
    out = pl.pallas_call(
        _fourier2_kernel,
        grid=(BATCH, n_j),
        in_specs=[
            pl.BlockSpec((None, FFT_A, TN_F2), lambda b, j: (b, 0, j)),
            pl.BlockSpec((None, FFT_A, TN_F2), lambda b, j: (b, 0, j + n_j)),
            _const_spec((FFT_A, FFT_A)),
            _const_spec((FFT_A, FFT_A)),
        ],
        out_specs=pl.BlockSpec((None, FFT_A, TN_F2), lambda b, j: (b, 0, j)),
        out_shape=jax.ShapeDtypeStruct((BATCH, FFT_A, half), F32),
        compiler_params=_params(32),
        name="fourier2",
    )(z2, z2, _C128, _S128)
    return out.reshape(N_X, D_FOURIER)


def _fourier_ctx_kernel(w_ref, c_ref, s_ref, o_ref):
    w = w_ref[...]
    o_ref[...] = (jnp.dot(c_ref[...], w[:, 0:D_FOURIER], precision=HIGHEST, preferred_element_type=F32)
                  + jnp.dot(s_ref[...], w[:, D_FOURIER:], precision=HIGHEST, preferred_element_type=F32)
                  ) * FOURIER_SCALE_C


def _fourier_ctx_call(w):
    return pl.pallas_call(
        _fourier_ctx_kernel,
        grid=(BATCH,),
        in_specs=[
            pl.BlockSpec((CTX_LEN, 2 * D_FOURIER), lambda b: (N_X // CTX_LEN + b, 0)),
            _const_spec((CTX_LEN, CTX_LEN)),
            _const_spec((CTX_LEN, CTX_LEN)),
        ],
        out_specs=pl.BlockSpec((CTX_LEN, D_FOURIER), lambda b: (b, 0)),
        out_shape=jax.ShapeDtypeStruct((N_C, D_FOURIER), F32),
        name="fourier_ctx",
    )(w, _C256, _S256)


def _proj_odd_kernel(x_ref, mod_ref, g_ref, w_ref, b_ref, u_ref):
    h = _norm_mod(x_ref[...], g_ref[...], mod_ref[0:1, :], mod_ref[1:2, :]).astype(BF16)
    a = jnp.dot(h, w_ref[...], preferred_element_type=F32) + b_ref[...]
    u_ref[...] = a[:, 0:D_MODEL] * jax.nn.sigmoid(a[:, D_MODEL:])


def _proj_odd_call(tok, mod, g, w, b, n_blocks):
    return pl.pallas_call(
        _proj_odd_kernel,
        grid=(n_blocks,),
        in_specs=[
            pl.BlockSpec((TM, D_MODEL), lambda i: (i, 0)),
            pl.BlockSpec((None, N_MOD, D_MODEL), lambda i: (_mod_block(i), 0, 0)),
            _const_spec((1, D_MODEL)),
            _const_spec((D_MODEL, 2 * D_MODEL)),
            _const_spec((1, 2 * D_MODEL)),
        ],
        out_specs=pl.BlockSpec((TM, D_MODEL), lambda i: (i, 0)),
        out_shape=jax.ShapeDtypeStruct((n_blocks * TM, D_MODEL), F32),
        compiler_params=_params(48),
        name="proj_odd",
    )(tok, mod, g, w, b)


def _dwconv_kernel(prev_ref, cur_ref, next_ref, w_ref, b_ref, lng_ref, lnb_ref, o_ref,
                   ext_scr, conv_scr):
    i = pl.program_id(0)
    blocks_per_seq = SEQ // CONV_TM
    is_ctx = i >= N_X // CONV_TM
    first = jnp.logical_or(is_ctx, i % blocks_per_seq == 0)
    last = jnp.logical_or(is_ctx, i % blocks_per_seq == blocks_per_seq - 1)
    ext_scr[0:CONV_HALO, :] = jnp.where(first, 0.0, prev_ref[...])
    ext_scr[CONV_HALO:CONV_HALO + CONV_TM, :] = cur_ref[...]
    ext_scr[CONV_HALO + CONV_TM:, :] = jnp.where(last, 0.0, next_ref[...])
    off = CONV_HALO - PAD
    for lc in range(D_MODEL // LANES):
        lanes = slice(LANES * lc, LANES * (lc + 1))
        for base in range(0, CONV_TM, CONV_RC):
            acc = jnp.broadcast_to(b_ref[:, lanes], (CONV_RC, LANES))
            for k in range(CONV_WIDTH):
                lo = base + off + k
                acc = acc + w_ref[k:k + 1, lanes] * ext_scr[lo:lo + CONV_RC, lanes]
            conv_scr[base:base + CONV_RC, lanes] = acc
    u = conv_scr[...]
    mu = jnp.mean(u, axis=-1, keepdims=True)
    d = u - mu
    var = jnp.mean(d * d, axis=-1, keepdims=True)
    y = d * lax.rsqrt(var + EPS) * lng_ref[...] + lnb_ref[...]
    o_ref[...] = (y * jax.nn.sigmoid(y)).astype(BF16)


def _dwconv_call(u, w_dw, b_dw, ln_g, ln_b):
    n_rows = u.shape[0]
    n_blocks = n_rows // CONV_TM
    halo_per_block = CONV_TM // CONV_HALO
    n_halo = n_rows // CONV_HALO
    return pl.pallas_call(
        _dwconv_kernel,
        grid=(n_blocks,),
        in_specs=[
            pl.BlockSpec((CONV_HALO, D_MODEL), lambda i: (jnp.maximum(i * halo_per_block - 1, 0), 0)),
            pl.BlockSpec((CONV_TM, D_MODEL), lambda i: (i, 0)),
            pl.BlockSpec((CONV_HALO, D_MODEL),
                         lambda i: (jnp.minimum((i + 1) * halo_per_block, n_halo - 1), 0)),
            _const_spec((CONV_WIDTH, D_MODEL)),
            _const_spec((1, D_MODEL)),
            _const_spec((1, D_MODEL)),
            _const_spec((1, D_MODEL)),
        ],
        out_specs=pl.BlockSpec((CONV_TM, D_MODEL), lambda i: (i, 0)),
        out_shape=jax.ShapeDtypeStruct((n_rows, D_MODEL), BF16),
        scratch_shapes=[pltpu.VMEM((CONV_TM + 2 * CONV_HALO, D_MODEL), F32),
                        pltpu.VMEM((CONV_TM, D_MODEL), F32)],
        compiler_params=_params(32),
        name="dwconv",
    )(u, u, u, w_dw, b_dw, ln_g, ln_b)


def _mlp_tail(x1, mod_ref, g2_ref, w1_ref, w2_ref, out_ref):
    h2 = _norm_mod(x1, g2_ref[...], mod_ref[3:4, :], mod_ref[4:5, :]).astype(BF16)
    acc = jnp.zeros((TM, D_MODEL), F32)
    for ck in range(D_FF // FF_CHUNK):
        cols = slice(FF_CHUNK * ck, FF_CHUNK * (ck + 1))
        a = jnp.maximum(jnp.dot(h2, w1_ref[:, cols], preferred_element_type=F32), 0.0)
        acc = acc + jnp.dot((a * a).astype(BF16), w2_ref[cols, :], preferred_element_type=F32)
    out_ref[...] = x1 + mod_ref[5:6, :] * acc


def _post_even_kernel(x_ref, mod_ref, o_ref, fx_ref, fc_ref, woa_ref, wof_ref, g2_ref, w1_ref, w2_ref,
                      out_ref):
    i = pl.program_id(0)
    fo = jnp.where(i >= X_BLOCKS, fc_ref[...], fx_ref[...]).astype(BF16)
    y = (jnp.dot(o_ref[...], woa_ref[...], preferred_element_type=F32)
         + jnp.dot(fo, wof_ref[...], preferred_element_type=F32))
    x1 = x_ref[...] + mod_ref[2:3, :] * y
    _mlp_tail(x1, mod_ref, g2_ref, w1_ref, w2_ref, out_ref)


def _post_odd_kernel(x_ref, mod_ref, v_ref, wo_ref, bo_ref, g2_ref, w1_ref, w2_ref, out_ref):
    y = jnp.dot(v_ref[...], wo_ref[...], preferred_element_type=F32) + bo_ref[...]
    x1 = x_ref[...] + mod_ref[2:3, :] * y
    _mlp_tail(x1, mod_ref, g2_ref, w1_ref, w2_ref, out_ref)


def _resident(shape):
    nd = len(shape)
    return pl.BlockSpec(shape, lambda *_: (0,) * nd, pipeline_mode=pl.Buffered(1))


def _post_even_call(tok, mod, o, f_x, f_c, wo_a, wo_f, g2, w1, w2):
    return pl.pallas_call(
        _post_even_kernel,
        grid=(TOK_BLOCKS,),
        in_specs=[
            pl.BlockSpec((TM, D_MODEL), lambda i: (i, 0)),
            pl.BlockSpec((None, N_MOD, D_MODEL), lambda i: (_mod_block(i), 0, 0)),
            pl.BlockSpec((TM, D_Q), lambda i: (i, 0)),
            pl.BlockSpec((TM, D_FOURIER), lambda i: (jnp.minimum(i, X_BLOCKS - 1), 0)),
            _const_spec((N_C, D_FOURIER)),
            _resident((D_Q, D_MODEL)),
            _resident((D_FOURIER, D_MODEL)),
            _const_spec((1, D_MODEL)),
            _resident((D_MODEL, D_FF)),
            _resident((D_FF, D_MODEL)),
        ],
        out_specs=pl.BlockSpec((TM, D_MODEL), lambda i: (i, 0)),
        out_shape=jax.ShapeDtypeStruct((N_TOK, D_MODEL), F32),
        compiler_params=_params(56),
        name="post_even",
    )(tok, mod, o, f_x, f_c, wo_a, wo_f, g2, w1, w2)


def _post_odd_call(tok, mod, v, wo, bo, g2, w1, w2, n_blocks):
    return pl.pallas_call(
        _post_odd_kernel,
        grid=(n_blocks,),
        in_specs=[
            pl.BlockSpec((TM, D_MODEL), lambda i: (i, 0)),
            pl.BlockSpec((None, N_MOD, D_MODEL), lambda i: (_mod_block(i), 0, 0)),
            pl.BlockSpec((TM, D_MODEL), lambda i: (i, 0)),
            _resident((D_MODEL, D_MODEL)),
            _const_spec((1, D_MODEL)),
            _const_spec((1, D_MODEL)),
            _resident((D_MODEL, D_FF)),
            _resident((D_FF, D_MODEL)),
        ],
        out_specs=pl.BlockSpec((TM, D_MODEL), lambda i: (i, 0)),
        out_shape=jax.ShapeDtypeStruct((n_blocks * TM, D_MODEL), F32),
        compiler_params=_params(56),
        name="post_odd",
    )(tok, mod, v, wo, bo, g2, w1, w2)


def _split_halves(w_cols):
    lead = w_cols.shape[:-1]
    n_heads = w_cols.shape[-1] // HEAD_DIM
    w4 = w_cols.reshape(*lead, n_heads, HALF, 2)
    return jnp.swapaxes(w4, -1, -2).reshape(*lead, n_heads * HEAD_DIM)


def kernel(x, c, ctx, c_ctx, ada_w, ada_b, norm1_g, norm2_g, mlp_w1, mlp_w2, attn_w_in, q_norm_g, k_norm_g, attn_w_out, conv_w_pw1, conv_b_pw1, conv_w_dw, conv_b_dw, conv_ln_g, conv_ln_b, conv_w_pw2, conv_b_pw2):
    tok = jnp.concatenate([x.reshape(N_X, D_MODEL), ctx.reshape(N_C, D_MODEL)], axis=0)
    c8 = jnp.concatenate([c, c_ctx[None, :], jnp.zeros((SUBLANES - BATCH - 1, D_MODEL), F32)], axis=0)
    mods = _ada_call(c8, ada_w, ada_b)[:, :BATCH + 1].reshape(DEPTH, BATCH + 1, N_MOD, D_MODEL)

    for i in range(DEPTH):
        j = i // 2
        last = i == DEPTH - 1
        n_blocks = X_BLOCKS if last else TOK_BLOCKS
        mod = mods[i]
        g1 = norm1_g[i][None, :]
        g2 = norm2_g[i][None, :]
        w1 = mlp_w1[i].astype(BF16)
        w2 = mlp_w2[i].astype(BF16)
        if i % 2 == 0:
            w_in = attn_w_in[j]
            wqkvt = jnp.concatenate([_split_halves(w_in[:, :D_Q + D_KV]),
                                     w_in[:, D_Q + D_KV:D_Q + 2 * D_KV]], axis=1).T.astype(BF16)
            wf = w_in[:, D_Q + 2 * D_KV:].astype(BF16)
            gain = jnp.concatenate([jnp.tile(_split_halves(q_norm_g[j]), N_Q_HEADS),
                                    jnp.tile(_split_halves(k_norm_g[j]), N_KV_HEADS)])[:, None]
            qt, k, vt3, w = _proj_even_call(tok, mod, g1, wqkvt, wf, gain)
            o = _attn_call(qt, k, vt3)
            f_x = _fourier2_call(_fourier1_call(w))
            f_c = _fourier_ctx_call(w)
            w_out = attn_w_out[j].astype(BF16)
            tok = _post_even_call(tok, mod, o, f_x, f_c, w_out[:D_Q], w_out[D_Q:], g2, w1, w2)
        else:
            u = _proj_odd_call(tok, mod, g1, conv_w_pw1[j].astype(BF16), conv_b_pw1[j][None, :], n_blocks)
            v = _dwconv_call(u, conv_w_dw[j], conv_b_dw[j][None, :], conv_ln_g[j][None, :],
                             conv_ln_b[j][None, :])
            tok = _post_odd_call(tok, mod, v, conv_w_pw2[j].astype(BF16), conv_b_pw2[j][None, :],
                                 g2, w1, w2, n_blocks)
    return tok[:N_X].reshape(BATCH, SEQ, D_MODEL)
```

```python
import functools

import numpy as np
import jax
import jax.numpy as jnp
from jax import lax
from jax.experimental import pallas as pl
from jax.experimental.pallas import tpu as pltpu

D_MODEL = 1024
BATCH = 2
SEQ = 8192
DEPTH = 4
GRID_W = 64
CTX_LEN = 256
HEAD_DIM = 64
N_Q_HEADS = 12
N_KV_HEADS = 4
GQA_GROUP = N_Q_HEADS // N_KV_HEADS
D_Q = N_Q_HEADS * HEAD_DIM
D_KV = N_KV_HEADS * HEAD_DIM
D_FOURIER = 256
FOURIER_GROUP_DIM = 64
ROPE_PAIRS_PER_AXIS = HEAD_DIM // 4
ROPE_THETA = 10000.0
CONV_WIDTH = 31
D_FF = 4 * D_MODEL
N_MOD = 6
EPS = 1e-6
ATTN_SCALE = HEAD_DIM ** -0.5
LOG2_E = 1.4426950408889634
Q_SCALE = ATTN_SCALE * LOG2_E

N_X = BATCH * SEQ
N_C = BATCH * CTX_LEN
N_TOK = N_X + N_C
N_HEADS_QK = N_Q_HEADS + N_KV_HEADS
HALF = HEAD_DIM // 2

LANES = 128
SUBLANES = 8
MIB = 1024 * 1024

TM = 512
X_BLOCKS = N_X // TM
TOK_BLOCKS = N_TOK // TM
TM_E = CTX_LEN
TQ = CTX_LEN
NQ = SEQ // TQ
K_ROWS = SEQ + CTX_LEN
KV_BLOCKS = K_ROWS // TM_E
CH = 3 * TM_E
V_ROWS = HEAD_DIM + 16
MAX_SCORE_BOUND = 50.0
FF_CHUNK = 512
TN_ADA = 1536
FFT_A = 128
FFT_B = SEQ // FFT_A
TA = 8
TN_F2 = 2048
CONV_TM = CTX_LEN
CONV_HALO = 16
CONV_RC = 64
PAD = CONV_WIDTH // 2

F32 = jnp.float32
BF16 = jnp.bfloat16
HIGHEST = lax.Precision.HIGHEST


def _rope_tables_t():
    freqs = (np.float32(ROPE_THETA) ** (-np.arange(ROPE_PAIRS_PER_AXIS, dtype=np.float32)
                                        / np.float32(ROPE_PAIRS_PER_AXIS))).astype(np.float32)
    t = np.arange(SEQ)
    row = (t // GRID_W).astype(np.float32)
    col = (t % GRID_W).astype(np.float32)
    ang = np.concatenate([row[:, None] * freqs, col[:, None] * freqs], axis=-1).astype(np.float32)
    cos = np.cos(ang.astype(np.float64)).T
    sin = np.sin(ang.astype(np.float64)).T
    cos = np.concatenate([cos, cos, np.ones((HALF, N_C))], axis=1)
    sin = np.concatenate([sin, sin, np.zeros((HALF, N_C))], axis=1)
    return cos.astype(np.float32), sin.astype(np.float32)


def _dft_cos_sin(n_out, n_in, period):
    k = np.arange(n_out)[:, None].astype(np.int64)
    n = np.arange(n_in)[None, :].astype(np.int64)
    ang = 2.0 * np.pi * ((k * n) % period).astype(np.float64) / period
    return np.cos(ang), np.sin(ang)


def _channel_dft_tables():
    c, s = _dft_cos_sin(FOURIER_GROUP_DIM, FOURIER_GROUP_DIM, FOURIER_GROUP_DIM)
    n_groups = D_FOURIER // FOURIER_GROUP_DIM
    cc = np.kron(np.eye(n_groups), c)
    sc = np.kron(np.eye(n_groups), s)
    return cc.astype(np.float32), sc.astype(np.float32)


def _stage1_tables():
    bp = np.arange(FFT_B)[None, :, None].astype(np.int64)
    a = np.arange(FFT_A)[:, None, None].astype(np.int64)
    b = np.arange(FFT_B)[None, None, :].astype(np.int64)
    ang = 2.0 * np.pi * ((bp * (a + FFT_A * b)) % SEQ).astype(np.float64) / SEQ
    return np.concatenate([np.cos(ang), np.sin(ang)], axis=1).astype(np.float32)


_ROPE_COS_T, _ROPE_SIN_T = _rope_tables_t()
_CC, _SC = _channel_dft_tables()
_STAGE1 = _stage1_tables()
_C128, _S128 = (t.astype(np.float32) for t in _dft_cos_sin(FFT_A, FFT_A, FFT_A))
_C256, _S256 = (t.astype(np.float32) for t in _dft_cos_sin(CTX_LEN, CTX_LEN, CTX_LEN))
FOURIER_SCALE_X = float((SEQ * FOURIER_GROUP_DIM) ** -0.5)
FOURIER_SCALE_C = float((CTX_LEN * FOURIER_GROUP_DIM) ** -0.5)


def _mod_block(i, tm=TM):
    return jnp.minimum(i // (SEQ // tm), BATCH)


def _const_spec(shape):
    nd = len(shape)
    return pl.BlockSpec(shape, lambda *_: (0,) * nd)


def _norm_mod(x, g, shift, scale):
    ms = jnp.mean(x * x, axis=-1, keepdims=True)
    return (x * lax.rsqrt(ms + EPS) * g) * (1.0 + scale) + shift


def _params(vmem_mib):
    return pltpu.CompilerParams(vmem_limit_bytes=vmem_mib * MIB)


def _ada_kernel(c_ref, w_ref, b_ref, o_ref):
    c = c_ref[...]
    s = c * jax.nn.sigmoid(c)
    o_ref[...] = jnp.dot(s.astype(BF16), w_ref[...].astype(BF16),
                         preferred_element_type=F32) + b_ref[...]


def _ada_call(c8, ada_w, ada_b):
    n_out = N_MOD * D_MODEL
    return pl.pallas_call(
        _ada_kernel,
        grid=(DEPTH, n_out // TN_ADA),
        in_specs=[
            _const_spec((SUBLANES, D_MODEL)),
            pl.BlockSpec((None, D_MODEL, TN_ADA), lambda l, j: (l, 0, j)),
            pl.BlockSpec((None, 1, TN_ADA), lambda l, j: (l, 0, j)),
        ],
        out_specs=pl.BlockSpec((None, SUBLANES, TN_ADA), lambda l, j: (l, 0, j)),
        out_shape=jax.ShapeDtypeStruct((DEPTH, SUBLANES, n_out), F32),
        compiler_params=_params(40),
        name="ada",
    )(c8, ada_w, ada_b.reshape(DEPTH, 1, n_out))


def _kv_block(i):
    n_x = N_X // TM_E
    per_batch = SEQ // TM_E
    return jnp.where(i < n_x, (i // per_batch) * KV_BLOCKS + i % per_batch,
                     (i - n_x) * KV_BLOCKS + per_batch)


def _proj_even_kernel(x_ref, mod_ref, g_ref, wqkvt_ref, wf_ref, gain_ref, cos_ref, sin_ref,
                      cc_ref, sc_ref, qt_ref, k_ref, vt_ref, w_ref, kt_scr):
    h = _norm_mod(x_ref[...], g_ref[...], mod_ref[0:1, :], mod_ref[1:2, :]).astype(BF16)
    t = lax.dot_general(wqkvt_ref[...], h, (((1,), (1,)), ((), ())), preferred_element_type=F32)
    cos = cos_ref[...]
    sin = sin_ref[...]
    for hd in range(N_HEADS_QK):
        lo = HEAD_DIM * hd
        x1 = t[lo:lo + HALF]
        x2 = t[lo + HALF:lo + HEAD_DIM]
        ms = (jnp.sum(x1 * x1, axis=0, keepdims=True)
              + jnp.sum(x2 * x2, axis=0, keepdims=True)) * (1.0 / HEAD_DIM)
        r = lax.rsqrt(ms + EPS)
        x1 = x1 * r * gain_ref[lo:lo + HALF, :]
        x2 = x2 * r * gain_ref[lo + HALF:lo + HEAD_DIM, :]
        o1 = x1 * cos - x2 * sin
        o2 = x1 * sin + x2 * cos
        if hd < N_Q_HEADS:
            qt_ref[lo:lo + HALF, :] = (o1 * Q_SCALE).astype(BF16)
            qt_ref[lo + HALF:lo + HEAD_DIM, :] = (o2 * Q_SCALE).astype(BF16)
        else:
            kt_scr[lo - D_Q:lo - D_Q + HALF, :] = o1
            kt_scr[lo - D_Q + HALF:lo - D_Q + HEAD_DIM, :] = o2
    k_ref[...] = kt_scr[...].T.astype(BF16)
    ones_rows = (lax.broadcasted_iota(jnp.int32, (V_ROWS - HEAD_DIM, TM_E), 0) == 0).astype(BF16)
    for g in range(N_KV_HEADS):
        src = D_Q + D_KV + HEAD_DIM * g
        vt_ref[V_ROWS * g:V_ROWS * g + HEAD_DIM, :] = t[src:src + HEAD_DIM].astype(BF16)
        vt_ref[V_ROWS * g + HEAD_DIM:V_ROWS * (g + 1), :] = ones_rows
    f = jnp.dot(h, wf_ref[...], preferred_element_type=F32)
    w_ref[:, 0:D_FOURIER] = jnp.dot(f, cc_ref[...], precision=HIGHEST, preferred_element_type=F32)
    w_ref[:, D_FOURIER:2 * D_FOURIER] = -jnp.dot(f, sc_ref[...], precision=HIGHEST,
                                                 preferred_element_type=F32)


def _proj_even_call(tok, mod, g, wqkvt, wf, gain):
    d_qkv = D_Q + 2 * D_KV
    return pl.pallas_call(
        _proj_even_kernel,
        grid=(N_TOK // TM_E,),
        in_specs=[
            pl.BlockSpec((TM_E, D_MODEL), lambda i: (i, 0)),
            pl.BlockSpec((None, N_MOD, D_MODEL), lambda i: (_mod_block(i, TM_E), 0, 0)),
            _const_spec((1, D_MODEL)),
            _const_spec((d_qkv, D_MODEL)),
            _const_spec((D_MODEL, D_FOURIER)),
            _const_spec((D_Q + D_KV, 1)),
            pl.BlockSpec((HALF, TM_E), lambda i: (0, i)),
            pl.BlockSpec((HALF, TM_E), lambda i: (0, i)),
            _const_spec((D_FOURIER, D_FOURIER)),
            _const_spec((D_FOURIER, D_FOURIER)),
        ],
        out_specs=[
            pl.BlockSpec((D_Q, TM_E), lambda i: (0, i)),
            pl.BlockSpec((TM_E, D_KV), lambda i: (_kv_block(i), 0)),
            pl.BlockSpec((None, N_KV_HEADS * V_ROWS, TM_E), lambda i: (_kv_block(i), 0, 0)),
            pl.BlockSpec((TM_E, 2 * D_FOURIER), lambda i: (i, 0)),
        ],
        out_shape=[
            jax.ShapeDtypeStruct((D_Q, N_TOK), BF16),
            jax.ShapeDtypeStruct((BATCH * K_ROWS, D_KV), BF16),
            jax.ShapeDtypeStruct((BATCH * KV_BLOCKS, N_KV_HEADS * V_ROWS, TM_E), BF16),
            jax.ShapeDtypeStruct((N_TOK, 2 * D_FOURIER), F32),
        ],
        scratch_shapes=[pltpu.VMEM((D_KV, TM_E), F32)],
        compiler_params=_params(48),
        name="proj_even",
    )(tok, mod, g, wqkvt, wf, gain, _ROPE_COS_T, _ROPE_SIN_T, _CC, _SC)


def _attn_kernel(par_ref, qt_ref, k_ref, vt_ref, o_ref, ot_scr):
    qi = pl.program_id(1)
    is_latent = qi < NQ
    use_bound = par_ref[0] > 0.5
    k_norm_bound = par_ref[1]
    blocks_per_chunk = CH // TM_E

    def group_consts(g):
        pair, side = divmod(g, 2)
        lanes = slice(LANES * pair, LANES * (pair + 1))
        rows = slice(V_ROWS * g, V_ROWS * (g + 1))
        qpads, qnorms = [], []
        for hh in range(GQA_GROUP):
            row0 = (GQA_GROUP * g + hh) * HEAD_DIM
            q_h = qt_ref[row0:row0 + HEAD_DIM, :]
            zero = jnp.zeros_like(q_h)
            qpads.append(jnp.concatenate([q_h, zero] if side == 0 else [zero, q_h], axis=0))
            qf = q_h.astype(F32)
            qnorms.append(jnp.sqrt(jnp.sum(qf * qf, axis=0, keepdims=True)))
        return lanes, rows, qpads, qnorms

    def pv(vt_blocks, p, acc):
        for j, vt in enumerate(vt_blocks):
            acc = acc + jnp.dot(vt, p[TM_E * j:TM_E * (j + 1), :], preferred_element_type=F32)
        return acc

    def store_heads(g, accs):
        for hh, acc in enumerate(accs):
            row0 = (GQA_GROUP * g + hh) * HEAD_DIM
            ot_scr[row0:row0 + HEAD_DIM, :] = acc[0:HEAD_DIM] / acc[HEAD_DIM:HEAD_DIM + 1]

    def online_step(k_chunk, vt_blocks, qpads, carries):
        sts = [jnp.dot(k_chunk, qpad, preferred_element_type=F32) for qpad in qpads]
        mid = []
        for st, (m, acc) in zip(sts, carries):
            m_new = jnp.maximum(m, jnp.max(st, axis=0, keepdims=True))
            mid.append((m_new, jnp.exp2(m - m_new), jnp.exp2(st - m_new).astype(BF16)))
        return tuple((m_new, pv(vt_blocks, p, alpha * acc))
                     for (m_new, alpha, p), (_, acc) in zip(mid, carries))

    def chunk_operands(c, lanes, rows):
        k0 = pl.multiple_of(c * CH, CH)
        return (k_ref[pl.ds(k0, CH), lanes],
                [vt_ref[c * blocks_per_chunk + j, rows, :] for j in range(blocks_per_chunk)])

    zero_acc = jnp.zeros((V_ROWS, TQ), F32)
    online_init = (jnp.full((1, TQ), -jnp.inf, F32), zero_acc)

    @pl.when(jnp.logical_and(is_latent, use_bound))
    def _():
        for g in range(N_KV_HEADS):
            lanes, rows, qpads, qnorms = group_consts(g)
            bounds = [qn * k_norm_bound for qn in qnorms]

            def body(c, accs, lanes=lanes, rows=rows, qpads=qpads, bounds=bounds):
                k_chunk, vt_blocks = chunk_operands(c, lanes, rows)
                sts = [jnp.dot(k_chunk, qpad, preferred_element_type=F32) for qpad in qpads]
                ps = [jnp.exp2(st - u).astype(BF16) for st, u in zip(sts, bounds)]
                return tuple(pv(vt_blocks, p, acc) for p, acc in zip(ps, accs))

            accs = lax.fori_loop(0, K_ROWS // CH, body, (zero_acc,) * GQA_GROUP)
            store_heads(g, accs)

    @pl.when(jnp.logical_and(is_latent, jnp.logical_not(use_bound)))
    def _():
        for g in range(N_KV_HEADS):
            lanes, rows, qpads, _ = group_consts(g)

            def body(c, carries, lanes=lanes, rows=rows, qpads=qpads):
                k_chunk, vt_blocks = chunk_operands(c, lanes, rows)
                return online_step(k_chunk, vt_blocks, qpads, carries)

            carries = lax.fori_loop(0, K_ROWS // CH, body, (online_init,) * GQA_GROUP)
            store_heads(g, [acc for _, acc in carries])

    @pl.when(jnp.logical_not(is_latent))
    def _():
        for g in range(N_KV_HEADS):
            lanes, rows, qpads, _ = group_consts(g)
            carries = online_step(k_ref[SEQ:K_ROWS, lanes], [vt_ref[KV_BLOCKS - 1, rows, :]], qpads,
                                  (online_init,) * GQA_GROUP)
            store_heads(g, [acc for _, acc in carries])

    o_ref[...] = ot_scr[...].T.astype(BF16)


def _attn_call(par, qt, k, vt3):
    def q_block(b, qi):
        return jnp.where(qi < NQ, b * NQ + qi, N_X // TQ + b)

    return pl.pallas_call(
        _attn_kernel,
        grid=(BATCH, NQ + 1),
        in_specs=[
            pl.BlockSpec(memory_space=pltpu.SMEM),
            pl.BlockSpec((D_Q, TQ), lambda b, qi: (0, q_block(b, qi))),
            pl.BlockSpec((K_ROWS, D_KV), lambda b, qi: (b, 0)),
            pl.BlockSpec((KV_BLOCKS, N_KV_HEADS * V_ROWS, TM_E), lambda b, qi: (b, 0, 0)),
        ],
        out_specs=pl.BlockSpec((TQ, D_Q), lambda b, qi: (q_block(b, qi), 0)),
        out_shape=jax.ShapeDtypeStruct((N_TOK, D_Q), BF16),
        scratch_shapes=[pltpu.VMEM((D_Q, TQ), F32)],
        compiler_params=_params(48),
        name="attn",
    )(par, qt, k, vt3)


def _attn_params(q_gain, k_gain):
    slack = 1.0 + 2.0 ** -6
    k_norm_bound = (HEAD_DIM ** 0.5) * jnp.max(jnp.abs(k_gain)) * slack
    q_norm_bound = (HEAD_DIM ** 0.5) * jnp.max(jnp.abs(q_gain)) * Q_SCALE * slack
    use_bound = (q_norm_bound * k_norm_bound < MAX_SCORE_BOUND).astype(F32)
    return jnp.stack([use_bound, k_norm_bound]).astype(F32)


def _fourier1_kernel(w_ref, m_ref, z_ref):
    for j in range(TA):
        xa = w_ref[:, j, :]
        y = jnp.dot(m_ref[j], xa, precision=HIGHEST, preferred_element_type=F32)
        z_ref[j, 0:FFT_B, :] = y[0:FFT_B, 0:D_FOURIER] + y[FFT_B:, D_FOURIER:]
        z_ref[j, FFT_B:, :] = y[0:FFT_B, D_FOURIER:] - y[FFT_B:, 0:D_FOURIER]


def _fourier1_call(w):
    w3 = w.reshape(N_TOK // FFT_A, FFT_A, 2 * D_FOURIER)
    return pl.pallas_call(
        _fourier1_kernel,
        grid=(BATCH, FFT_A // TA),
        in_specs=[
            pl.BlockSpec((FFT_B, TA, 2 * D_FOURIER), lambda b, a: (b, a, 0)),
            pl.BlockSpec((TA, 2 * FFT_B, FFT_B), lambda b, a: (a, 0, 0)),
        ],
        out_specs=pl.BlockSpec((None, TA, 2 * FFT_B, D_FOURIER), lambda b, a: (b, a, 0, 0)),
        out_shape=jax.ShapeDtypeStruct((BATCH, FFT_A, 2 * FFT_B, D_FOURIER), F32),
        compiler_params=_params(32),
        name="fourier1",
    )(w3, _STAGE1)


def _fourier2_kernel(zr_ref, zi_ref, c_ref, s_ref, o_ref):
    o_ref[...] = (jnp.dot(c_ref[...], zr_ref[...], precision=HIGHEST, preferred_element_type=F32)
                  + jnp.dot(s_ref[...], zi_ref[...], precision=HIGHEST, preferred_element_type=F32)
                  ) * FOURIER_SCALE_X


def _fourier2_call(z):
    half = FFT_B * D_FOURIER
    z2 = z.reshape(BATCH, FFT_A, 2 * half)
    n_j = half // TN_F2
    out = pl.pallas_call(
        _fourier2_kernel,
        grid=(BATCH, n_j),
        in_specs=[
            pl.BlockSpec((None, FFT_A, TN_F2), lambda b, j: (b, 0, j)),
            pl.BlockSpec((None, FFT_A, TN_F2), lambda b, j: (b, 0, j + n_j)),
            _const_spec((FFT_A, FFT_A)),
            _const_spec((FFT_A, FFT_A)),
        ],
        out_specs=pl.BlockSpec((None, FFT_A, TN_F2), lambda b, j: (b, 0, j)),
        out_shape=jax.ShapeDtypeStruct((BATCH, FFT_A, half), F32),
        compiler_params=_params(32),
        name="fourier2",
    )(z2, z2, _C128, _S128)
    return out.reshape(N_X, D_FOURIER)


def _fourier_ctx_kernel(w_ref, c_ref, s_ref, o_ref):
    w = w_ref[...]
    o_ref[...] = (jnp.dot(c_ref[...], w[:, 0:D_FOURIER], precision=HIGHEST, preferred_element_type=F32)
                  + jnp.dot(s_ref[...], w[:, D_FOURIER:], precision=HIGHEST, preferred_element_type=F32)
                  ) * FOURIER_SCALE_C


def _fourier_ctx_call(w):
    return pl.pallas_call(
        _fourier_ctx_kernel,
        grid=(BATCH,),
        in_specs=[
            pl.BlockSpec((CTX_LEN, 2 * D_FOURIER), lambda b: (N_X // CTX_LEN + b, 0)),
            _const_spec((CTX_LEN, CTX_LEN)),
            _const_spec((CTX_LEN, CTX_LEN)),
        ],
        out_specs=pl.BlockSpec((CTX_LEN, D_FOURIER), lambda b: (b, 0)),
        out_shape=jax.ShapeDtypeStruct((N_C, D_FOURIER), F32),
        name="fourier_ctx",
    )(w, _C256, _S256)


def _proj_odd_kernel(x_ref, mod_ref, g_ref, w_ref, b_ref, u_ref):
    h = _norm_mod(x_ref[...], g_ref[...], mod_ref[0:1, :], mod_ref[1:2, :]).astype(BF16)
    a = jnp.dot(h, w_ref[...], preferred_element_type=F32) + b_ref[...]
    u_ref[...] = a[:, 0:D_MODEL] * jax.nn.sigmoid(a[:, D_MODEL:])


def _proj_odd_call(tok, mod, g, w, b, n_blocks):
    return pl.pallas_call(
        _proj_odd_kernel,
        grid=(n_blocks,),
        in_specs=[
            pl.BlockSpec((TM, D_MODEL), lambda i: (i, 0)),
            pl.BlockSpec((None, N_MOD, D_MODEL), lambda i: (_mod_block(i), 0, 0)),
            _const_spec((1, D_MODEL)),
            _const_spec((D_MODEL, 2 * D_MODEL)),
            _const_spec((1, 2 * D_MODEL)),
        ],
        out_specs=pl.BlockSpec((TM, D_MODEL), lambda i: (i, 0)),
        out_shape=jax.ShapeDtypeStruct((n_blocks * TM, D_MODEL), F32),
        compiler_params=_params(48),
        name="proj_odd",
    )(tok, mod, g, w, b)


def _dwconv_kernel(prev_ref, cur_ref, next_ref, w_ref, b_ref, lng_ref, lnb_ref, o_ref,
                   ext_scr, conv_scr):
    i = pl.program_id(0)
    blocks_per_seq = SEQ // CONV_TM
    is_ctx = i >= N_X // CONV_TM
    first = jnp.logical_or(is_ctx, i % blocks_per_seq == 0)
    last = jnp.logical_or(is_ctx, i % blocks_per_seq == blocks_per_seq - 1)
    ext_scr[0:CONV_HALO, :] = jnp.where(first, 0.0, prev_ref[...])
    ext_scr[CONV_HALO:CONV_HALO + CONV_TM, :] = cur_ref[...]
    ext_scr[CONV_HALO + CONV_TM:, :] = jnp.where(last, 0.0, next_ref[...])
    off = CONV_HALO - PAD
    for lc in range(D_MODEL // LANES):
        lanes = slice(LANES * lc, LANES * (lc + 1))
        for base in range(0, CONV_TM, CONV_RC):
            acc = jnp.broadcast_to(b_ref[:, lanes], (CONV_RC, LANES))
            for k in range(CONV_WIDTH):
                lo = base + off + k
                acc = acc + w_ref[k:k + 1, lanes] * ext_scr[lo:lo + CONV_RC, lanes]
            conv_scr[base:base + CONV_RC, lanes] = acc
    u = conv_scr[...]
    mu = jnp.mean(u, axis=-1, keepdims=True)
    d = u - mu
    var = jnp.mean(d * d, axis=-1, keepdims=True)
    y = d * lax.rsqrt(var + EPS) * lng_ref[...] + lnb_ref[...]
    o_ref[...] = (y * jax.nn.sigmoid(y)).astype(BF16)


def _dwconv_call(u, w_dw, b_dw, ln_g, ln_b):
    n_rows = u.shape[0]
    n_blocks = n_rows // CONV_TM
    halo_per_block = CONV_TM // CONV_HALO
    n_halo = n_rows // CONV_HALO
    return pl.pallas_call(
        _dwconv_kernel,
        grid=(n_blocks,),
        in_specs=[
            pl.BlockSpec((CONV_HALO, D_MODEL), lambda i: (jnp.maximum(i * halo_per_block - 1, 0), 0)),
            pl.BlockSpec((CONV_TM, D_MODEL), lambda i: (i, 0)),
            pl.BlockSpec((CONV_HALO, D_MODEL),
                         lambda i: (jnp.minimum((i + 1) * halo_per_block, n_halo - 1), 0)),
            _const_spec((CONV_WIDTH, D_MODEL)),
            _const_spec((1, D_MODEL)),
            _const_spec((1, D_MODEL)),
            _const_spec((1, D_MODEL)),
        ],
        out_specs=pl.BlockSpec((CONV_TM, D_MODEL), lambda i: (i, 0)),
        out_shape=jax.ShapeDtypeStruct((n_rows, D_MODEL), BF16),
        scratch_shapes=[pltpu.VMEM((CONV_TM + 2 * CONV_HALO, D_MODEL), F32),
                        pltpu.VMEM((CONV_TM, D_MODEL), F32)],
        compiler_params=_params(32),
        name="dwconv",
    )(u, u, u, w_dw, b_dw, ln_g, ln_b)


def _mlp_tail(x1, mod_ref, g2_ref, w1_ref, w2_ref, out_ref):
    h2 = _norm_mod(x1, g2_ref[...], mod_ref[3:4, :], mod_ref[4:5, :]).astype(BF16)
    acc = jnp.zeros((TM, D_MODEL), F32)
    for ck in range(D_FF // FF_CHUNK):
        cols = slice(FF_CHUNK * ck, FF_CHUNK * (ck + 1))
        a = jnp.maximum(jnp.dot(h2, w1_ref[:, cols], preferred_element_type=F32), 0.0)
        acc = acc + jnp.dot((a * a).astype(BF16), w2_ref[cols, :], preferred_element_type=F32)
    out_ref[...] = x1 + mod_ref[5:6, :] * acc


def _post_even_kernel(x_ref, mod_ref, o_ref, fx_ref, fc_ref, woa_ref, wof_ref, g2_ref, w1_ref, w2_ref,
                      out_ref):
    i = pl.program_id(0)
    fo = jnp.where(i >= X_BLOCKS, fc_ref[...], fx_ref[...]).astype(BF16)
    y = (jnp.dot(o_ref[...], woa_ref[...], preferred_element_type=F32)
         + jnp.dot(fo, wof_ref[...], preferred_element_type=F32))
    x1 = x_ref[...] + mod_ref[2:3, :] * y
    _mlp_tail(x1, mod_ref, g2_ref, w1_ref, w2_ref, out_ref)


def _post_odd_kernel(x_ref, mod_ref, v_ref, wo_ref, bo_ref, g2_ref, w1_ref, w2_ref, out_ref):
    y = jnp.dot(v_ref[...], wo_ref[...], preferred_element_type=F32) + bo_ref[...]
    x1 = x_ref[...] + mod_ref[2:3, :] * y
    _mlp_tail(x1, mod_ref, g2_ref, w1_ref, w2_ref, out_ref)


def _resident(shape):
    nd = len(shape)
    return pl.BlockSpec(shape, lambda *_: (0,) * nd, pipeline_mode=pl.Buffered(1))


def _post_even_call(tok, mod, o, f_x, f_c, wo_a, wo_f, g2, w1, w2):
    return pl.pallas_call(
        _post_even_kernel,
        grid=(TOK_BLOCKS,),
        in_specs=[
            pl.BlockSpec((TM, D_MODEL), lambda i: (i, 0)),
            pl.BlockSpec((None, N_MOD, D_MODEL), lambda i: (_mod_block(i), 0, 0)),
            pl.BlockSpec((TM, D_Q), lambda i: (i, 0)),
            pl.BlockSpec((TM, D_FOURIER), lambda i: (jnp.minimum(i, X_BLOCKS - 1), 0)),
            _const_spec((N_C, D_FOURIER)),
            _resident((D_Q, D_MODEL)),
            _resident((D_FOURIER, D_MODEL)),
            _const_spec((1, D_MODEL)),
            _resident((D_MODEL, D_FF)),
            _resident((D_FF, D_MODEL)),
        ],
        out_specs=pl.BlockSpec((TM, D_MODEL), lambda i: (i, 0)),
        out_shape=jax.ShapeDtypeStruct((N_TOK, D_MODEL), F32),
        compiler_params=_params(56),
        name="post_even",
    )(tok, mod, o, f_x, f_c, wo_a, wo_f, g2, w1, w2)


def _post_odd_call(tok, mod, v, wo, bo, g2, w1, w2, n_blocks):
    return pl.pallas_call(
        _post_odd_kernel,
        grid=(n_blocks,),
        in_specs=[
            pl.BlockSpec((TM, D_MODEL), lambda i: (i, 0)),
            pl.BlockSpec((None, N_MOD, D_MODEL), lambda i: (_mod_block(i), 0, 0)),
            pl.BlockSpec((TM, D_MODEL), lambda i: (i, 0)),
            _resident((D_MODEL, D_MODEL)),
            _const_spec((1, D_MODEL)),
            _const_spec((1, D_MODEL)),
            _resident((D_MODEL, D_FF)),
            _resident((D_FF, D_MODEL)),
        ],
        out_specs=pl.BlockSpec((TM, D_MODEL), lambda i: (i, 0)),
        out_shape=jax.ShapeDtypeStruct((n_blocks * TM, D_MODEL), F32),
        compiler_params=_params(56),
        name="post_odd",
    )(tok, mod, v, wo, bo, g2, w1, w2)


def _split_halves(w_cols):
    lead = w_cols.shape[:-1]
    n_heads = w_cols.shape[-1] // HEAD_DIM
    w4 = w_cols.reshape(*lead, n_heads, HALF, 2)
    return jnp.swapaxes(w4, -1, -2).reshape(*lead, n_heads * HEAD_DIM)


def kernel(x, c, ctx, c_ctx, ada_w, ada_b, norm1_g, norm2_g, mlp_w1, mlp_w2, attn_w_in, q_norm_g, k_norm_g, attn_w_out, conv_w_pw1, conv_b_pw1, conv_w_dw, conv_b_dw, conv_ln_g, conv_ln_b, conv_w_pw2, conv_b_pw2):
    tok = jnp.concatenate([x.reshape(N_X, D_MODEL), ctx.reshape(N_C, D_MODEL)], axis=0)
    c8 = jnp.concatenate([c, c_ctx[None, :], jnp.zeros((SUBLANES - BATCH - 1, D_MODEL), F32)], axis=0)
    mods = _ada_call(c8, ada_w, ada_b)[:, :BATCH + 1].reshape(DEPTH, BATCH + 1, N_MOD, D_MODEL)

    for i in range(DEPTH):
        j = i // 2
        last = i == DEPTH - 1
        n_blocks = X_BLOCKS if last else TOK_BLOCKS
        mod = mods[i]
        g1 = norm1_g[i][None, :]
        g2 = norm2_g[i][None, :]
        w1 = mlp_w1[i].astype(BF16)
        w2 = mlp_w2[i].astype(BF16)
        if i % 2 == 0:
            w_in = attn_w_in[j]
            wqkvt = jnp.concatenate([_split_halves(w_in[:, :D_Q + D_KV]),
                                     w_in[:, D_Q + D_KV:D_Q + 2 * D_KV]], axis=1).T.astype(BF16)
            wf = w_in[:, D_Q + 2 * D_KV:].astype(BF16)
            gain = jnp.concatenate([jnp.tile(_split_halves(q_norm_g[j]), N_Q_HEADS),
                                    jnp.tile(_split_halves(k_norm_g[j]), N_KV_HEADS)])[:, None]
            qt, k, vt3, w = _proj_even_call(tok, mod, g1, wqkvt, wf, gain)
            o = _attn_call(_attn_params(q_norm_g[j], k_norm_g[j]), qt, k, vt3)
            f_x = _fourier2_call(_fourier1_call(w))
            f_c = _fourier_ctx_call(w)
            w_out = attn_w_out[j].astype(BF16)
            tok = _post_even_call(tok, mod, o, f_x, f_c, w_out[:D_Q], w_out[D_Q:], g2, w1, w2)
        else:
            u = _proj_odd_call(tok, mod, g1, conv_w_pw1[j].astype(BF16), conv_b_pw1[j][None, :], n_blocks)
            v = _dwconv_call(u, conv_w_dw[j], conv_b_dw[j][None, :], conv_ln_g[j][None, :],
                             conv_ln_b[j][None, :])
            tok = _post_odd_call(tok, mod, v, conv_w_pw2[j].astype(BF16), conv_b_pw2[j][None, :],
                                 g2, w1, w2, n_blocks)
    return tok[:N_X].reshape(BATCH, SEQ, D_MODEL)
```

```python
import functools

import numpy as np
import jax
import jax.numpy as jnp
from jax import lax
from jax.experimental import pallas as pl
from jax.experimental.pallas import tpu as pltpu

D_MODEL = 1024
BATCH = 2
SEQ = 8192
DEPTH = 4
GRID_W = 64
CTX_LEN = 256
HEAD_DIM = 64
N_Q_HEADS = 12
N_KV_HEADS = 4
GQA_GROUP = N_Q_HEADS // N_KV_HEADS
D_Q = N_Q_HEADS * HEAD_DIM
D_KV = N_KV_HEADS * HEAD_DIM
D_FOURIER = 256
FOURIER_GROUP_DIM = 64
ROPE_PAIRS_PER_AXIS = HEAD_DIM // 4
ROPE_THETA = 10000.0
CONV_WIDTH = 31
D_FF = 4 * D_MODEL
N_MOD = 6
EPS = 1e-6
ATTN_SCALE = HEAD_DIM ** -0.5
LOG2_E = 1.4426950408889634
Q_SCALE = ATTN_SCALE * LOG2_E

N_X = BATCH * SEQ
N_C = BATCH * CTX_LEN
N_TOK = N_X + N_C
N_HEADS_QK = N_Q_HEADS + N_KV_HEADS
HALF = HEAD_DIM // 2

LANES = 128
SUBLANES = 8
MIB = 1024 * 1024

TM = 512
X_BLOCKS = N_X // TM
TOK_BLOCKS = N_TOK // TM
TM_E = CTX_LEN
TQ = CTX_LEN
NQ = SEQ // TQ
K_ROWS = SEQ + CTX_LEN
KV_BLOCKS = K_ROWS // TM_E
CH = 11 * TM_E
V_ROWS = HEAD_DIM + 16
MAX_SCORE_BOUND = 50.0
FF_CHUNK = 512
TN_ADA = 1536
FFT_A = 128
FFT_B = SEQ // FFT_A
TA = 8
TB2 = 8
CONV_TM = CTX_LEN
CONV_HALO = 16
CONV_RC = 64
PAD = CONV_WIDTH // 2

F32 = jnp.float32
BF16 = jnp.bfloat16
HIGHEST = lax.Precision.HIGHEST


def _rope_tables_t():
    freqs = (np.float32(ROPE_THETA) ** (-np.arange(ROPE_PAIRS_PER_AXIS, dtype=np.float32)
                                        / np.float32(ROPE_PAIRS_PER_AXIS))).astype(np.float32)
    t = np.arange(SEQ)
    row = (t // GRID_W).astype(np.float32)
    col = (t % GRID_W).astype(np.float32)
    ang = np.concatenate([row[:, None] * freqs, col[:, None] * freqs], axis=-1).astype(np.float32)
    cos = np.cos(ang.astype(np.float64)).T
    sin = np.sin(ang.astype(np.float64)).T
    cos = np.concatenate([cos, cos, np.ones((HALF, N_C))], axis=1)
    sin = np.concatenate([sin, sin, np.zeros((HALF, N_C))], axis=1)
    return cos.astype(np.float32), sin.astype(np.float32)


def _dft_cos_sin(n_out, n_in, period):
    k = np.arange(n_out)[:, None].astype(np.int64)
    n = np.arange(n_in)[None, :].astype(np.int64)
    ang = 2.0 * np.pi * ((k * n) % period).astype(np.float64) / period
    return np.cos(ang), np.sin(ang)


def _channel_dft_tables():
    c, s = _dft_cos_sin(FOURIER_GROUP_DIM, FOURIER_GROUP_DIM, FOURIER_GROUP_DIM)
    n_groups = D_FOURIER // FOURIER_GROUP_DIM
    cc = np.kron(np.eye(n_groups), c)
    sc = np.kron(np.eye(n_groups), s)
    return cc.astype(np.float32), sc.astype(np.float32)


def _stage1_tables():
    bp = np.arange(FFT_B)[None, :, None].astype(np.int64)
    a = np.arange(FFT_A)[:, None, None].astype(np.int64)
    b = np.arange(FFT_B)[None, None, :].astype(np.int64)
    ang = 2.0 * np.pi * ((bp * (a + FFT_A * b)) % SEQ).astype(np.float64) / SEQ
    return np.concatenate([np.cos(ang), np.sin(ang)], axis=1).astype(np.float32)


_ROPE_COS_T, _ROPE_SIN_T = _rope_tables_t()
_CC, _SC = _channel_dft_tables()
_STAGE1 = _stage1_tables()
_C128, _S128 = (t.astype(np.float32) for t in _dft_cos_sin(FFT_A, FFT_A, FFT_A))
_C256, _S256 = (t.astype(np.float32) for t in _dft_cos_sin(CTX_LEN, CTX_LEN, CTX_LEN))
FOURIER_SCALE_X = float((SEQ * FOURIER_GROUP_DIM) ** -0.5)
FOURIER_SCALE_C = float((CTX_LEN * FOURIER_GROUP_DIM) ** -0.5)


def _mod_block(i, tm=TM):
    return jnp.minimum(i // (SEQ // tm), BATCH)


def _const_spec(shape):
    nd = len(shape)
    return pl.BlockSpec(shape, lambda *_: (0,) * nd)


def _norm_mod(x, g, shift, scale):
    ms = jnp.mean(x * x, axis=-1, keepdims=True)
    return (x * lax.rsqrt(ms + EPS) * g) * (1.0 + scale) + shift


def _params(vmem_mib):
    return pltpu.CompilerParams(vmem_limit_bytes=vmem_mib * MIB)


def _ada_kernel(c_ref, w_ref, b_ref, o_ref):
    c = c_ref[...]
    s = c * jax.nn.sigmoid(c)
    o_ref[...] = jnp.dot(s.astype(BF16), w_ref[...].astype(BF16),
                         preferred_element_type=F32) + b_ref[...]


def _ada_call(c8, ada_w, ada_b):
    n_out = N_MOD * D_MODEL
    return pl.pallas_call(
        _ada_kernel,
        grid=(DEPTH, n_out // TN_ADA),
        in_specs=[
            _const_spec((SUBLANES, D_MODEL)),
            pl.BlockSpec((None, D_MODEL, TN_ADA), lambda l, j: (l, 0, j)),
            pl.BlockSpec((None, 1, TN_ADA), lambda l, j: (l, 0, j)),
        ],
        out_specs=pl.BlockSpec((None, SUBLANES, TN_ADA), lambda l, j: (l, 0, j)),
        out_shape=jax.ShapeDtypeStruct((DEPTH, SUBLANES, n_out), F32),
        compiler_params=_params(40),
        name="ada",
    )(c8, ada_w, ada_b.reshape(DEPTH, 1, n_out))


def _kv_block(i):
    n_x = N_X // TM_E
    per_batch = SEQ // TM_E
    return jnp.where(i < n_x, (i // per_batch) * KV_BLOCKS + i % per_batch,
                     (i - n_x) * KV_BLOCKS + per_batch)


def _token_block(refs, n_tok_refs, n_latent_blocks):
    if n_tok_refs == 1:
        return refs[0][...]
    return jnp.where(pl.program_id(0) >= n_latent_blocks, refs[1][...], refs[0][...])


def _token_specs(tok, tm):
    if not isinstance(tok, tuple):
        return (tok,), [pl.BlockSpec((tm, D_MODEL), lambda i: (i, 0))]
    n_x = N_X // tm
    return tok, [pl.BlockSpec((tm, D_MODEL), lambda i: (jnp.minimum(i, n_x - 1), 0)),
                 pl.BlockSpec((tm, D_MODEL), lambda i: (jnp.maximum(i - n_x, 0), 0))]


def _proj_even_kernel(*refs, n_tok_refs):
    (mod_ref, g_ref, wqkvt_ref, wf_ref, gain_ref, cos_ref, sin_ref, cc_ref, sc_ref,
     qt_ref, k_ref, vt_ref, w_ref, kt_scr) = refs[n_tok_refs:]
    x = _token_block(refs, n_tok_refs, N_X // TM_E)
    h = _norm_mod(x, g_ref[...], mod_ref[0:1, :], mod_ref[1:2, :]).astype(BF16)
    t = lax.dot_general(wqkvt_ref[...], h, (((1,), (1,)), ((), ())), preferred_element_type=F32)
    cos = cos_ref[...]
    sin = sin_ref[...]
    for hd in range(N_HEADS_QK):
        lo = HEAD_DIM * hd
        x1 = t[lo:lo + HALF]
        x2 = t[lo + HALF:lo + HEAD_DIM]
        ms = (jnp.sum(x1 * x1, axis=0, keepdims=True)
              + jnp.sum(x2 * x2, axis=0, keepdims=True)) * (1.0 / HEAD_DIM)
        r = lax.rsqrt(ms + EPS)
        x1 = x1 * r * gain_ref[lo:lo + HALF, :]
        x2 = x2 * r * gain_ref[lo + HALF:lo + HEAD_DIM, :]
        o1 = x1 * cos - x2 * sin
        o2 = x1 * sin + x2 * cos
        if hd < N_Q_HEADS:
            qt_ref[lo:lo + HALF, :] = (o1 * Q_SCALE).astype(BF16)
            qt_ref[lo + HALF:lo + HEAD_DIM, :] = (o2 * Q_SCALE).astype(BF16)
        else:
            kt_scr[lo - D_Q:lo - D_Q + HALF, :] = o1
            kt_scr[lo - D_Q + HALF:lo - D_Q + HEAD_DIM, :] = o2
    k_ref[...] = kt_scr[...].T.astype(BF16)
    ones_rows = (lax.broadcasted_iota(jnp.int32, (V_ROWS - HEAD_DIM, TM_E), 0) == 0).astype(BF16)
    for g in range(N_KV_HEADS):
        src = D_Q + D_KV + HEAD_DIM * g
        vt_ref[V_ROWS * g:V_ROWS * g + HEAD_DIM, :] = t[src:src + HEAD_DIM].astype(BF16)
        vt_ref[V_ROWS * g + HEAD_DIM:V_ROWS * (g + 1), :] = ones_rows
    f = jnp.dot(h, wf_ref[...], preferred_element_type=F32)
    w_ref[:, 0:D_FOURIER] = jnp.dot(f, cc_ref[...], precision=HIGHEST, preferred_element_type=F32)
    w_ref[:, D_FOURIER:2 * D_FOURIER] = -jnp.dot(f, sc_ref[...], precision=HIGHEST,
                                                 preferred_element_type=F32)


def _proj_even_call(tok, mod, g, wqkvt, wf, gain):
    d_qkv = D_Q + 2 * D_KV
    toks, tok_specs = _token_specs(tok, TM_E)
    return pl.pallas_call(
        functools.partial(_proj_even_kernel, n_tok_refs=len(toks)),
        grid=(N_TOK // TM_E,),
        in_specs=tok_specs + [
            pl.BlockSpec((None, N_MOD, D_MODEL), lambda i: (_mod_block(i, TM_E), 0, 0)),
            _const_spec((1, D_MODEL)),
            _const_spec((d_qkv, D_MODEL)),
            _const_spec((D_MODEL, D_FOURIER)),
            _const_spec((D_Q + D_KV, 1)),
            pl.BlockSpec((HALF, TM_E), lambda i: (0, i)),
            pl.BlockSpec((HALF, TM_E), lambda i: (0, i)),
            _const_spec((D_FOURIER, D_FOURIER)),
            _const_spec((D_FOURIER, D_FOURIER)),
        ],
        out_specs=[
            pl.BlockSpec((D_Q, TM_E), lambda i: (0, i)),
            pl.BlockSpec((TM_E, D_KV), lambda i: (_kv_block(i), 0)),
            pl.BlockSpec((None, N_KV_HEADS * V_ROWS, TM_E), lambda i: (_kv_block(i), 0, 0)),
            pl.BlockSpec((TM_E, 2 * D_FOURIER), lambda i: (i, 0)),
        ],
        out_shape=[
            jax.ShapeDtypeStruct((D_Q, N_TOK), BF16),
            jax.ShapeDtypeStruct((BATCH * K_ROWS, D_KV), BF16),
            jax.ShapeDtypeStruct((BATCH * KV_BLOCKS, N_KV_HEADS * V_ROWS, TM_E), BF16),
            jax.ShapeDtypeStruct((N_TOK, 2 * D_FOURIER), F32),
        ],
        scratch_shapes=[pltpu.VMEM((D_KV, TM_E), F32)],
        compiler_params=_params(48),
        name="proj_even",
    )(*toks, mod, g, wqkvt, wf, gain, _ROPE_COS_T, _ROPE_SIN_T, _CC, _SC)


def _attn_kernel(par_ref, qt_ref, k_ref, vt_ref, o_ref, ot_scr):
    qi = pl.program_id(1)
    is_latent = qi < NQ
    use_bound = par_ref[0] > 0.5
    k_norm_bound = par_ref[1]
    blocks_per_chunk = CH // TM_E

    def group_consts(g):
        pair, side = divmod(g, 2)
        lanes = slice(LANES * pair, LANES * (pair + 1))
        rows = slice(V_ROWS * g, V_ROWS * (g + 1))
        qpads, qnorms = [], []
        for hh in range(GQA_GROUP):
            row0 = (GQA_GROUP * g + hh) * HEAD_DIM
            q_h = qt_ref[row0:row0 + HEAD_DIM, :]
            zero = jnp.zeros_like(q_h)
            qpads.append(jnp.concatenate([q_h, zero] if side == 0 else [zero, q_h], axis=0))
            qf = q_h.astype(F32)
            qnorms.append(jnp.sqrt(jnp.sum(qf * qf, axis=0, keepdims=True)))
        return lanes, rows, qpads, qnorms

    def pv(vt_blocks, p, acc):
        for j, vt in enumerate(vt_blocks):
            acc = acc + jnp.dot(vt, p[TM_E * j:TM_E * (j + 1), :], preferred_element_type=F32)
        return acc

    def store_heads(g, accs):
        for hh, acc in enumerate(accs):
            row0 = (GQA_GROUP * g + hh) * HEAD_DIM
            ot_scr[row0:row0 + HEAD_DIM, :] = acc[0:HEAD_DIM] / acc[HEAD_DIM:HEAD_DIM + 1]

    def online_step(k_chunk, vt_blocks, qpads, carries):
        sts = [jnp.dot(k_chunk, qpad, preferred_element_type=F32) for qpad in qpads]
        mid = []
        for st, (m, acc) in zip(sts, carries):
            m_new = jnp.maximum(m, jnp.max(st, axis=0, keepdims=True))
            mid.append((m_new, jnp.exp2(m - m_new), jnp.exp2(st - m_new).astype(BF16)))
        return tuple((m_new, pv(vt_blocks, p, alpha * acc))
                     for (m_new, alpha, p), (_, acc) in zip(mid, carries))

    def chunk_operands(c, lanes, rows):
        k0 = pl.multiple_of(c * CH, CH)
        return (k_ref[pl.ds(k0, CH), lanes],
                [vt_ref[c * blocks_per_chunk + j, rows, :] for j in range(blocks_per_chunk)])

    zero_acc = jnp.zeros((V_ROWS, TQ), F32)
    online_init = (jnp.full((1, TQ), -jnp.inf, F32), zero_acc)

    @pl.when(jnp.logical_and(is_latent, use_bound))
    def _():
        for g in range(N_KV_HEADS):
            lanes, rows, qpads, qnorms = group_consts(g)
            bounds = [qn * k_norm_bound for qn in qnorms]

            def body(c, accs, lanes=lanes, rows=rows, qpads=qpads, bounds=bounds):
                k_chunk, vt_blocks = chunk_operands(c, lanes, rows)
                sts = [jnp.dot(k_chunk, qpad, preferred_element_type=F32) for qpad in qpads]
                ps = [jnp.exp2(st - u).astype(BF16) for st, u in zip(sts, bounds)]
                return tuple(pv(vt_blocks, p, acc) for p, acc in zip(ps, accs))

            accs = lax.fori_loop(0, K_ROWS // CH, body, (zero_acc,) * GQA_GROUP)
            store_heads(g, accs)

    @pl.when(jnp.logical_and(is_latent, jnp.logical_not(use_bound)))
    def _():
        for g in range(N_KV_HEADS):
            lanes, rows, qpads, _ = group_consts(g)

            def body(c, carries, lanes=lanes, rows=rows, qpads=qpads):
                k_chunk, vt_blocks = chunk_operands(c, lanes, rows)
                return online_step(k_chunk, vt_blocks, qpads, carries)

            carries = lax.fori_loop(0, K_ROWS // CH, body, (online_init,) * GQA_GROUP)
            store_heads(g, [acc for _, acc in carries])

    @pl.when(jnp.logical_not(is_latent))
    def _():
        for g in range(N_KV_HEADS):
            lanes, rows, qpads, _ = group_consts(g)
            carries = online_step(k_ref[SEQ:K_ROWS, lanes], [vt_ref[KV_BLOCKS - 1, rows, :]], qpads,
                                  (online_init,) * GQA_GROUP)
            store_heads(g, [acc for _, acc in carries])

    o_ref[...] = ot_scr[...].T.astype(BF16)


def _attn_call(par, qt, k, vt3):
    def q_block(b, qi):
        return jnp.where(qi < NQ, b * NQ + qi, N_X // TQ + b)

    return pl.pallas_call(
        _attn_kernel,
        grid=(BATCH, NQ + 1),
        in_specs=[
            pl.BlockSpec(memory_space=pltpu.SMEM),
            pl.BlockSpec((D_Q, TQ), lambda b, qi: (0, q_block(b, qi))),
            pl.BlockSpec((K_ROWS, D_KV), lambda b, qi: (b, 0)),
            pl.BlockSpec((KV_BLOCKS, N_KV_HEADS * V_ROWS, TM_E), lambda b, qi: (b, 0, 0)),
        ],
        out_specs=pl.BlockSpec((TQ, D_Q), lambda b, qi: (q_block(b, qi), 0)),
        out_shape=jax.ShapeDtypeStruct((N_TOK, D_Q), BF16),
        scratch_shapes=[pltpu.VMEM((D_Q, TQ), F32)],
        compiler_params=_params(48),
        name="attn",
    )(par, qt, k, vt3)


def _attn_params(q_gain, k_gain):
    slack = 1.0 + 2.0 ** -6
    k_norm_bound = (HEAD_DIM ** 0.5) * jnp.max(jnp.abs(k_gain)) * slack
    q_norm_bound = (HEAD_DIM ** 0.5) * jnp.max(jnp.abs(q_gain)) * Q_SCALE * slack
    use_bound = (q_norm_bound * k_norm_bound < MAX_SCORE_BOUND).astype(F32)
    return jnp.stack([use_bound, k_norm_bound]).astype(F32)


def _fourier1_kernel(w_ref, m_ref, z_ref):
    for j in range(TA):
        xa = w_ref[:, j, :]
        y = jnp.dot(m_ref[j], xa, precision=HIGHEST, preferred_element_type=F32)
        z_ref[j, 0:FFT_B, :] = y[0:FFT_B, 0:D_FOURIER] + y[FFT_B:, D_FOURIER:]
        z_ref[j, FFT_B:, :] = y[0:FFT_B, D_FOURIER:] - y[FFT_B:, 0:D_FOURIER]


def _fourier1_call(w):
    w3 = w.reshape(N_TOK // FFT_A, FFT_A, 2 * D_FOURIER)
    return pl.pallas_call(
        _fourier1_kernel,
        grid=(BATCH, FFT_A // TA),
        in_specs=[
            pl.BlockSpec((FFT_B, TA, 2 * D_FOURIER), lambda b, a: (b, a, 0)),
            pl.BlockSpec((TA, 2 * FFT_B, FFT_B), lambda b, a: (a, 0, 0)),
        ],
        out_specs=pl.BlockSpec((None, TA, 2 * FFT_B, D_FOURIER), lambda b, a: (b, a, 0, 0)),
        out_shape=jax.ShapeDtypeStruct((BATCH, FFT_A, 2 * FFT_B, D_FOURIER), F32),
        compiler_params=_params(32),
        name="fourier1",
    )(w3, _STAGE1)


def _fourier2_kernel(zr_ref, zi_ref, c_ref, s_ref, o_ref):
    zr = jnp.concatenate([zr_ref[:, j, :] for j in range(TB2)], axis=1)
    zi = jnp.concatenate([zi_ref[:, j, :] for j in range(TB2)], axis=1)
    res = (jnp.dot(c_ref[...], zr, precision=HIGHEST, preferred_element_type=F32)
           + jnp.dot(s_ref[...], zi, precision=HIGHEST, preferred_element_type=F32)) * FOURIER_SCALE_X
    for j in range(TB2):
        o_ref[:, j, :] = res[:, D_FOURIER * j:D_FOURIER * (j + 1)]


def _fourier2_call(z):
    n_j = FFT_B // TB2
    out = pl.pallas_call(
        _fourier2_kernel,
        grid=(BATCH, n_j),
        in_specs=[
            pl.BlockSpec((None, FFT_A, TB2, D_FOURIER), lambda b, j: (b, 0, j, 0)),
            pl.BlockSpec((None, FFT_A, TB2, D_FOURIER), lambda b, j: (b, 0, j + n_j, 0)),
            _const_spec((FFT_A, FFT_A)),
            _const_spec((FFT_A, FFT_A)),
        ],
        out_specs=pl.BlockSpec((None, FFT_A, TB2, D_FOURIER), lambda b, j: (b, 0, j, 0)),
        out_shape=jax.ShapeDtypeStruct((BATCH, FFT_A, FFT_B, D_FOURIER), F32),
        compiler_params=_params(32),
        name="fourier2",
    )(z, z, _C128, _S128)
    return out.reshape(N_X, D_FOURIER)


def _fourier_ctx_kernel(w_ref, c_ref, s_ref, o_ref):
    w = w_ref[...]
    o_ref[...] = (jnp.dot(c_ref[...], w[:, 0:D_FOURIER], precision=HIGHEST, preferred_element_type=F32)
                  + jnp.dot(s_ref[...], w[:, D_FOURIER:], precision=HIGHEST, preferred_element_type=F32)
                  ) * FOURIER_SCALE_C


def _fourier_ctx_call(w):
    return pl.pallas_call(
        _fourier_ctx_kernel,
        grid=(BATCH,),
        in_specs=[
            pl.BlockSpec((CTX_LEN, 2 * D_FOURIER), lambda b: (N_X // CTX_LEN + b, 0)),
            _const_spec((CTX_LEN, CTX_LEN)),
            _const_spec((CTX_LEN, CTX_LEN)),
        ],
        out_specs=pl.BlockSpec((CTX_LEN, D_FOURIER), lambda b: (b, 0)),
        out_shape=jax.ShapeDtypeStruct((N_C, D_FOURIER), F32),
        name="fourier_ctx",
    )(w, _C256, _S256)


def _proj_odd_kernel(x_ref, mod_ref, g_ref, w_ref, b_ref, u_ref):
    h = _norm_mod(x_ref[...], g_ref[...], mod_ref[0:1, :], mod_ref[1:2, :]).astype(BF16)
    a = jnp.dot(h, w_ref[...], preferred_element_type=F32) + b_ref[...]
    u_ref[...] = a[:, 0:D_MODEL] * jax.nn.sigmoid(a[:, D_MODEL:])


def _proj_odd_call(tok, mod, g, w, b, n_blocks):
    return pl.pallas_call(
        _proj_odd_kernel,
        grid=(n_blocks,),
        in_specs=[
            pl.BlockSpec((TM, D_MODEL), lambda i: (i, 0)),
            pl.BlockSpec((None, N_MOD, D_MODEL), lambda i: (_mod_block(i), 0, 0)),
            _const_spec((1, D_MODEL)),
            _const_spec((D_MODEL, 2 * D_MODEL)),
            _const_spec((1, 2 * D_MODEL)),
        ],
        out_specs=pl.BlockSpec((TM, D_MODEL), lambda i: (i, 0)),
        out_shape=jax.ShapeDtypeStruct((n_blocks * TM, D_MODEL), F32),
        compiler_params=_params(48),
        name="proj_odd",
    )(tok, mod, g, w, b)


def _dwconv_kernel(prev_ref, cur_ref, next_ref, w_ref, b_ref, lng_ref, lnb_ref, o_ref,
                   ext_scr, conv_scr):
    i = pl.program_id(0)
    blocks_per_seq = SEQ // CONV_TM
    is_ctx = i >= N_X // CONV_TM
    first = jnp.logical_or(is_ctx, i % blocks_per_seq == 0)
    last = jnp.logical_or(is_ctx, i % blocks_per_seq == blocks_per_seq - 1)
    ext_scr[0:CONV_HALO, :] = jnp.where(first, 0.0, prev_ref[...])
    ext_scr[CONV_HALO:CONV_HALO + CONV_TM, :] = cur_ref[...]
    ext_scr[CONV_HALO + CONV_TM:, :] = jnp.where(last, 0.0, next_ref[...])
    off = CONV_HALO - PAD
    win = CONV_RC + SUBLANES
    n_m = (CONV_WIDTH - 1 + off) // SUBLANES + 1
    for lc in range(D_MODEL // LANES):
        lanes = slice(LANES * lc, LANES * (lc + 1))
        for base in range(0, CONV_TM, CONV_RC):
            acc = jnp.broadcast_to(b_ref[:, lanes], (CONV_RC, LANES))
            for r in range(SUBLANES):
                q = None
                for m in range(n_m):
                    k = SUBLANES * m + r - off
                    if 0 <= k < CONV_WIDTH:
                        lo = base + SUBLANES * m
                        term = w_ref[k:k + 1, lanes] * ext_scr[lo:lo + win, lanes]
                        q = term if q is None else q + term
                acc = acc + q[r:r + CONV_RC]
            conv_scr[base:base + CONV_RC, lanes] = acc
    u = conv_scr[...]
    mu = jnp.mean(u, axis=-1, keepdims=True)
    d = u - mu
    var = jnp.mean(d * d, axis=-1, keepdims=True)
    y = d * lax.rsqrt(var + EPS) * lng_ref[...] + lnb_ref[...]
    o_ref[...] = (y * jax.nn.sigmoid(y)).astype(BF16)


def _dwconv_call(u, w_dw, b_dw, ln_g, ln_b):
    n_rows = u.shape[0]
    n_blocks = n_rows // CONV_TM
    halo_per_block = CONV_TM // CONV_HALO
    n_halo = n_rows // CONV_HALO
    return pl.pallas_call(
        _dwconv_kernel,
        grid=(n_blocks,),
        in_specs=[
            pl.BlockSpec((CONV_HALO, D_MODEL), lambda i: (jnp.maximum(i * halo_per_block - 1, 0), 0)),
            pl.BlockSpec((CONV_TM, D_MODEL), lambda i: (i, 0)),
            pl.BlockSpec((CONV_HALO, D_MODEL),
                         lambda i: (jnp.minimum((i + 1) * halo_per_block, n_halo - 1), 0)),
            _const_spec((CONV_WIDTH, D_MODEL)),
            _const_spec((1, D_MODEL)),
            _const_spec((1, D_MODEL)),
            _const_spec((1, D_MODEL)),
        ],
        out_specs=pl.BlockSpec((CONV_TM, D_MODEL), lambda i: (i, 0)),
        out_shape=jax.ShapeDtypeStruct((n_rows, D_MODEL), BF16),
        scratch_shapes=[pltpu.VMEM((CONV_TM + 2 * CONV_HALO, D_MODEL), F32),
                        pltpu.VMEM((CONV_TM, D_MODEL), F32)],
        compiler_params=_params(32),
        name="dwconv",
    )(u, u, u, w_dw, b_dw, ln_g, ln_b)


def _mlp_tail(x1, mod_ref, g2_ref, w1_ref, w2_ref, out_ref):
    h2 = _norm_mod(x1, g2_ref[...], mod_ref[3:4, :], mod_ref[4:5, :]).astype(BF16)
    acc = jnp.zeros((TM, D_MODEL), F32)
    for ck in range(D_FF // FF_CHUNK):
        cols = slice(FF_CHUNK * ck, FF_CHUNK * (ck + 1))
        a = jnp.maximum(jnp.dot(h2, w1_ref[:, cols], preferred_element_type=F32), 0.0)
        acc = acc + jnp.dot((a * a).astype(BF16), w2_ref[cols, :], preferred_element_type=F32)
    out_ref[...] = x1 + mod_ref[5:6, :] * acc


def _post_even_kernel(*refs, n_tok_refs):
    (mod_ref, o_ref, fx_ref, fc_ref, woa_ref, wof_ref, g2_ref, w1_ref, w2_ref,
     out_ref) = refs[n_tok_refs:]
    i = pl.program_id(0)
    fo = jnp.where(i >= X_BLOCKS, fc_ref[...], fx_ref[...]).astype(BF16)
    y = (jnp.dot(o_ref[...], woa_ref[...], preferred_element_type=F32)
         + jnp.dot(fo, wof_ref[...], preferred_element_type=F32))
    x1 = _token_block(refs, n_tok_refs, X_BLOCKS) + mod_ref[2:3, :] * y
    _mlp_tail(x1, mod_ref, g2_ref, w1_ref, w2_ref, out_ref)


def _post_odd_kernel(x_ref, mod_ref, v_ref, wo_ref, bo_ref, g2_ref, w1_ref, w2_ref, out_ref):
    y = jnp.dot(v_ref[...], wo_ref[...], preferred_element_type=F32) + bo_ref[...]
    x1 = x_ref[...] + mod_ref[2:3, :] * y
    _mlp_tail(x1, mod_ref, g2_ref, w1_ref, w2_ref, out_ref)


def _resident(shape):
    nd = len(shape)
    return pl.BlockSpec(shape, lambda *_: (0,) * nd, pipeline_mode=pl.Buffered(1))


def _post_even_call(tok, mod, o, f_x, f_c, wo_a, wo_f, g2, w1, w2):
    toks, tok_specs = _token_specs(tok, TM)
    return pl.pallas_call(
        functools.partial(_post_even_kernel, n_tok_refs=len(toks)),
        grid=(TOK_BLOCKS,),
        in_specs=tok_specs + [
            pl.BlockSpec((None, N_MOD, D_MODEL), lambda i: (_mod_block(i), 0, 0)),
            pl.BlockSpec((TM, D_Q), lambda i: (i, 0)),
            pl.BlockSpec((TM, D_FOURIER), lambda i: (jnp.minimum(i, X_BLOCKS - 1), 0)),
            _const_spec((N_C, D_FOURIER)),
            _resident((D_Q, D_MODEL)),
            _resident((D_FOURIER, D_MODEL)),
            _const_spec((1, D_MODEL)),
            _resident((D_MODEL, D_FF)),
            _resident((D_FF, D_MODEL)),
        ],
        out_specs=pl.BlockSpec((TM, D_MODEL), lambda i: (i, 0)),
        out_shape=jax.ShapeDtypeStruct((N_TOK, D_MODEL), F32),
        compiler_params=_params(56),
        name="post_even",
    )(*toks, mod, o, f_x, f_c, wo_a, wo_f, g2, w1, w2)


def _post_odd_call(tok, mod, v, wo, bo, g2, w1, w2, n_blocks):
    return pl.pallas_call(
        _post_odd_kernel,
        grid=(n_blocks,),
        in_specs=[
            pl.BlockSpec((TM, D_MODEL), lambda i: (i, 0)),
            pl.BlockSpec((None, N_MOD, D_MODEL), lambda i: (_mod_block(i), 0, 0)),
            pl.BlockSpec((TM, D_MODEL), lambda i: (i, 0)),
            _resident((D_MODEL, D_MODEL)),
            _const_spec((1, D_MODEL)),
            _const_spec((1, D_MODEL)),
            _resident((D_MODEL, D_FF)),
            _resident((D_FF, D_MODEL)),
        ],
        out_specs=pl.BlockSpec((TM, D_MODEL), lambda i: (i, 0)),
        out_shape=jax.ShapeDtypeStruct((n_blocks * TM, D_MODEL), F32),
        compiler_params=_params(56),
        name="post_odd",
    )(tok, mod, v, wo, bo, g2, w1, w2)


def _split_halves(w_cols):
    lead = w_cols.shape[:-1]
    n_heads = w_cols.shape[-1] // HEAD_DIM
    w4 = w_cols.reshape(*lead, n_heads, HALF, 2)
    return jnp.swapaxes(w4, -1, -2).reshape(*lead, n_heads * HEAD_DIM)


def kernel(x, c, ctx, c_ctx, ada_w, ada_b, norm1_g, norm2_g, mlp_w1, mlp_w2, attn_w_in, q_norm_g, k_norm_g, attn_w_out, conv_w_pw1, conv_b_pw1, conv_w_dw, conv_b_dw, conv_ln_g, conv_ln_b, conv_w_pw2, conv_b_pw2):
    tok = (x.reshape(N_X, D_MODEL), ctx.reshape(N_C, D_MODEL))
    c8 = jnp.concatenate([c, c_ctx[None, :], jnp.zeros((SUBLANES - BATCH - 1, D_MODEL), F32)], axis=0)
    mods = _ada_call(c8, ada_w, ada_b)[:, :BATCH + 1].reshape(DEPTH, BATCH + 1, N_MOD, D_MODEL)

    for i in range(DEPTH):
        j = i // 2
        last = i == DEPTH - 1
        n_blocks = X_BLOCKS if last else TOK_BLOCKS
        mod = mods[i]
        g1 = norm1_g[i][None, :]
        g2 = norm2_g[i][None, :]
        w1 = mlp_w1[i].astype(BF16)
        w2 = mlp_w2[i].astype(BF16)
        if i % 2 == 0:
            w_in = attn_w_in[j]
            wqkvt = jnp.concatenate([_split_halves(w_in[:, :D_Q + D_KV]),
                                     w_in[:, D_Q + D_KV:D_Q + 2 * D_KV]], axis=1).T.astype(BF16)
            wf = w_in[:, D_Q + 2 * D_KV:].astype(BF16)
            gain = jnp.concatenate([jnp.tile(_split_halves(q_norm_g[j]), N_Q_HEADS),
                                    jnp.tile(_split_halves(k_norm_g[j]), N_KV_HEADS)])[:, None]
            qt, k, vt3, w = _proj_even_call(tok, mod, g1, wqkvt, wf, gain)
            o = _attn_call(_attn_params(q_norm_g[j], k_norm_g[j]), qt, k, vt3)
            f_x = _fourier2_call(_fourier1_call(w))
            f_c = _fourier_ctx_call(w)
            w_out = attn_w_out[j].astype(BF16)
            tok = _post_even_call(tok, mod, o, f_x, f_c, w_out[:D_Q], w_out[D_Q:], g2, w1, w2)
        else:
            u = _proj_odd_call(tok, mod, g1, conv_w_pw1[j].astype(BF16), conv_b_pw1[j][None, :], n_blocks)
            v = _dwconv_call(u, conv_w_dw[j], conv_b_dw[j][None, :], conv_ln_g[j][None, :],
                             conv_ln_b[j][None, :])
            tok = _post_odd_call(tok, mod, v, conv_w_pw2[j].astype(BF16), conv_b_pw2[j][None, :],
                                 g2, w1, w2, n_blocks)
    return tok[:N_X].reshape(BATCH, SEQ, D_MODEL)
```

```python
import functools

import ml_dtypes
import numpy as np
import jax
import jax.numpy as jnp
from jax import lax
from jax.experimental import pallas as pl
from jax.experimental.pallas import tpu as pltpu

D_MODEL = 1024
BATCH = 2
SEQ = 8192
DEPTH = 4
GRID_W = 64
CTX_LEN = 256
HEAD_DIM = 64
N_Q_HEADS = 12
N_KV_HEADS = 4
GQA_GROUP = N_Q_HEADS // N_KV_HEADS
D_Q = N_Q_HEADS * HEAD_DIM
D_KV = N_KV_HEADS * HEAD_DIM
D_FOURIER = 256
FOURIER_GROUP_DIM = 64
ROPE_PAIRS_PER_AXIS = HEAD_DIM // 4
ROPE_THETA = 10000.0
CONV_WIDTH = 31
D_FF = 4 * D_MODEL
N_MOD = 6
EPS = 1e-6
ATTN_SCALE = HEAD_DIM ** -0.5
LOG2_E = 1.4426950408889634
Q_SCALE = ATTN_SCALE * LOG2_E

N_X = BATCH * SEQ
N_C = BATCH * CTX_LEN
N_TOK = N_X + N_C
N_HEADS_QK = N_Q_HEADS + N_KV_HEADS
HALF = HEAD_DIM // 2

LANES = 128
SUBLANES = 8
MIB = 1024 * 1024

TM = 512
X_BLOCKS = N_X // TM
TOK_BLOCKS = N_TOK // TM
TM_E = CTX_LEN
TQ = CTX_LEN
NQ = SEQ // TQ
K_ROWS = SEQ + CTX_LEN
KV_BLOCKS = K_ROWS // TM_E
CH = 11 * TM_E
V_ROWS = HEAD_DIM + 16
MAX_SCORE_BOUND = 50.0
FF_CHUNK = 512
TN_ADA = 1536
FFT_A = 128
FFT_B = SEQ // FFT_A
TA = 8
TB2 = 8
CONV_TM = CTX_LEN
CONV_HALO = 16
CONV_RC = 64
PAD = CONV_WIDTH // 2

F32 = jnp.float32
BF16 = jnp.bfloat16
HIGHEST = lax.Precision.HIGHEST


def _rope_tables_t():
    freqs = (np.float32(ROPE_THETA) ** (-np.arange(ROPE_PAIRS_PER_AXIS, dtype=np.float32)
                                        / np.float32(ROPE_PAIRS_PER_AXIS))).astype(np.float32)
    t = np.arange(SEQ)
    row = (t // GRID_W).astype(np.float32)
    col = (t % GRID_W).astype(np.float32)
    ang = np.concatenate([row[:, None] * freqs, col[:, None] * freqs], axis=-1).astype(np.float32)
    cos = np.cos(ang.astype(np.float64)).T
    sin = np.sin(ang.astype(np.float64)).T
    cos = np.concatenate([cos, cos, np.ones((HALF, N_C))], axis=1)
    sin = np.concatenate([sin, sin, np.zeros((HALF, N_C))], axis=1)
    return cos.astype(np.float32), sin.astype(np.float32)


def _dft_cos_sin(n_out, n_in, period):
    k = np.arange(n_out)[:, None].astype(np.int64)
    n = np.arange(n_in)[None, :].astype(np.int64)
    ang = 2.0 * np.pi * ((k * n) % period).astype(np.float64) / period
    return np.cos(ang), np.sin(ang)


def _channel_dft_tables():
    c, s = _dft_cos_sin(FOURIER_GROUP_DIM, FOURIER_GROUP_DIM, FOURIER_GROUP_DIM)
    n_groups = D_FOURIER // FOURIER_GROUP_DIM
    cc = np.kron(np.eye(n_groups), c)
    sc = np.kron(np.eye(n_groups), s)
    return cc.astype(np.float32), sc.astype(np.float32)


def _stage1_tables():
    bp = np.arange(FFT_B)[None, :, None].astype(np.int64)
    a = np.arange(FFT_A)[:, None, None].astype(np.int64)
    b = np.arange(FFT_B)[None, None, :].astype(np.int64)
    ang = 2.0 * np.pi * ((bp * (a + FFT_A * b)) % SEQ).astype(np.float64) / SEQ
    return np.concatenate([np.cos(ang), np.sin(ang)], axis=1).astype(np.float32)


def _hi_lo(table):
    t32 = np.asarray(table, np.float32)
    hi = t32.astype(ml_dtypes.bfloat16)
    lo = (t32 - hi.astype(np.float32)).astype(ml_dtypes.bfloat16)
    return hi, lo


_ROPE_COS_T, _ROPE_SIN_T = _rope_tables_t()
_CC, _SC = (_hi_lo(t) for t in _channel_dft_tables())
_STAGE1 = _stage1_tables()
_C128, _S128 = (_hi_lo(t) for t in _dft_cos_sin(FFT_A, FFT_A, FFT_A))
_C256, _S256 = (_hi_lo(t) for t in _dft_cos_sin(CTX_LEN, CTX_LEN, CTX_LEN))
FOURIER_SCALE_X = float((SEQ * FOURIER_GROUP_DIM) ** -0.5)
FOURIER_SCALE_C = float((CTX_LEN * FOURIER_GROUP_DIM) ** -0.5)


def _mod_block(i, tm=TM):
    return jnp.minimum(i // (SEQ // tm), BATCH)


def _const_spec(shape):
    nd = len(shape)
    return pl.BlockSpec(shape, lambda *_: (0,) * nd)


def _norm_mod(x, g, shift, scale):
    ms = jnp.mean(x * x, axis=-1, keepdims=True)
    return (x * lax.rsqrt(ms + EPS) * g) * (1.0 + scale) + shift


def _split(x):
    hi = x.astype(BF16)
    return hi, (x - hi.astype(F32)).astype(BF16)


def _dot3(a, b):
    def d(p, q):
        return jnp.dot(p, q, preferred_element_type=F32)
    return d(a[0], b[0]) + (d(a[1], b[0]) + d(a[0], b[1]))


def _params(vmem_mib):
    return pltpu.CompilerParams(vmem_limit_bytes=vmem_mib * MIB)


def _ada_kernel(c_ref, w_ref, b_ref, o_ref):
    c = c_ref[...]
    s = c * jax.nn.sigmoid(c)
    o_ref[...] = jnp.dot(s.astype(BF16), w_ref[...].astype(BF16),
                         preferred_element_type=F32) + b_ref[...]


def _ada_call(c8, ada_w, ada_b):
    n_out = N_MOD * D_MODEL
    return pl.pallas_call(
        _ada_kernel,
        grid=(DEPTH, n_out // TN_ADA),
        in_specs=[
            _const_spec((SUBLANES, D_MODEL)),
            pl.BlockSpec((None, D_MODEL, TN_ADA), lambda l, j: (l, 0, j)),
            pl.BlockSpec((None, 1, TN_ADA), lambda l, j: (l, 0, j)),
        ],
        out_specs=pl.BlockSpec((None, SUBLANES, TN_ADA), lambda l, j: (l, 0, j)),
        out_shape=jax.ShapeDtypeStruct((DEPTH, SUBLANES, n_out), F32),
        compiler_params=_params(40),
        name="ada",
    )(c8, ada_w, ada_b.reshape(DEPTH, 1, n_out))


def _kv_block(i):
    n_x = N_X // TM_E
    per_batch = SEQ // TM_E
    return jnp.where(i < n_x, (i // per_batch) * KV_BLOCKS + i % per_batch,
                     (i - n_x) * KV_BLOCKS + per_batch)


def _token_block(refs, n_tok_refs, n_latent_blocks):
    if n_tok_refs == 1:
        return refs[0][...]
    return jnp.where(pl.program_id(0) >= n_latent_blocks, refs[1][...], refs[0][...])


def _token_specs(tok, tm):
    if not isinstance(tok, tuple):
        return (tok,), [pl.BlockSpec((tm, D_MODEL), lambda i: (i, 0))]
    n_x = N_X // tm
    return tok, [pl.BlockSpec((tm, D_MODEL), lambda i: (jnp.minimum(i, n_x - 1), 0)),
                 pl.BlockSpec((tm, D_MODEL), lambda i: (jnp.maximum(i - n_x, 0), 0))]


def _proj_even_kernel(*refs, n_tok_refs):
    (mod_ref, g_ref, win_ref, gain_ref, cos_ref, sin_ref, cch_ref, ccl_ref, sch_ref, scl_ref,
     qt_ref, k_ref, vt_ref, w_ref, kt_scr, wt32_scr, wqkvt_scr, wf_scr) = refs[n_tok_refs:]
    d_qk = D_Q + D_KV
    d_qkv = D_Q + 2 * D_KV

    @pl.when(pl.program_id(0) == 0)
    def _():
        wt = win_ref[:, 0:d_qkv].T
        wqkvt_scr[d_qk:d_qkv, :] = wt[d_qk:d_qkv, :].astype(BF16)
        for c in range(D_MODEL // LANES):
            cols = slice(LANES * c, LANES * (c + 1))
            wt32_scr[c] = wt[:, cols]
            for hd in range(N_HEADS_QK):
                lo = HEAD_DIM * hd
                wqkvt_scr[lo:lo + HALF, cols] = wt32_scr[c, pl.ds(lo, HALF, stride=2), :].astype(BF16)
                wqkvt_scr[lo + HALF:lo + HEAD_DIM, cols] = (
                    wt32_scr[c, pl.ds(lo + 1, HALF, stride=2), :].astype(BF16))
        wf_scr[...] = win_ref[:, d_qkv:].astype(BF16)

    x = _token_block(refs, n_tok_refs, N_X // TM_E)
    h = _norm_mod(x, g_ref[...], mod_ref[0:1, :], mod_ref[1:2, :]).astype(BF16)
    t = lax.dot_general(wqkvt_scr[...], h, (((1,), (1,)), ((), ())), preferred_element_type=F32)
    cos = cos_ref[...]
    sin = sin_ref[...]
    for hd in range(N_HEADS_QK):
        lo = HEAD_DIM * hd
        x1 = t[lo:lo + HALF]
        x2 = t[lo + HALF:lo + HEAD_DIM]
        ms = (jnp.sum(x1 * x1, axis=0, keepdims=True)
              + jnp.sum(x2 * x2, axis=0, keepdims=True)) * (1.0 / HEAD_DIM)
        r = lax.rsqrt(ms + EPS)
        x1 = x1 * r * gain_ref[lo:lo + HALF, :]
        x2 = x2 * r * gain_ref[lo + HALF:lo + HEAD_DIM, :]
        o1 = x1 * cos - x2 * sin
        o2 = x1 * sin + x2 * cos
        if hd < N_Q_HEADS:
            qt_ref[lo:lo + HALF, :] = (o1 * Q_SCALE).astype(BF16)
            qt_ref[lo + HALF:lo + HEAD_DIM, :] = (o2 * Q_SCALE).astype(BF16)
        else:
            kt_scr[lo - D_Q:lo - D_Q + HALF, :] = o1
            kt_scr[lo - D_Q + HALF:lo - D_Q + HEAD_DIM, :] = o2
    k_ref[...] = kt_scr[...].T.astype(BF16)
    ones_rows = (lax.broadcasted_iota(jnp.int32, (V_ROWS - HEAD_DIM, TM_E), 0) == 0).astype(BF16)
    for g in range(N_KV_HEADS):
        src = D_Q + D_KV + HEAD_DIM * g
        vt_ref[V_ROWS * g:V_ROWS * g + HEAD_DIM, :] = t[src:src + HEAD_DIM].astype(BF16)
        vt_ref[V_ROWS * g + HEAD_DIM:V_ROWS * (g + 1), :] = ones_rows
    f = _split(jnp.dot(h, wf_scr[...], preferred_element_type=F32))
    w_ref[:, 0:D_FOURIER] = _dot3(f, (cch_ref[...], ccl_ref[...]))
    w_ref[:, D_FOURIER:2 * D_FOURIER] = -_dot3(f, (sch_ref[...], scl_ref[...]))


def _proj_even_call(tok, mod, g, w_in, gain):
    d_qkv = D_Q + 2 * D_KV
    toks, tok_specs = _token_specs(tok, TM_E)
    return pl.pallas_call(
        functools.partial(_proj_even_kernel, n_tok_refs=len(toks)),
        grid=(N_TOK // TM_E,),
        in_specs=tok_specs + [
            pl.BlockSpec((None, N_MOD, D_MODEL), lambda i: (_mod_block(i, TM_E), 0, 0)),
            _const_spec((1, D_MODEL)),
            _resident((D_MODEL, d_qkv + D_FOURIER)),
            _const_spec((D_Q + D_KV, 1)),
            pl.BlockSpec((HALF, TM_E), lambda i: (0, i)),
            pl.BlockSpec((HALF, TM_E), lambda i: (0, i)),
        ] + [_const_spec((D_FOURIER, D_FOURIER))] * 4,
        out_specs=[
            pl.BlockSpec((D_Q, TM_E), lambda i: (0, i)),
            pl.BlockSpec((TM_E, D_KV), lambda i: (_kv_block(i), 0)),
            pl.BlockSpec((None, N_KV_HEADS * V_ROWS, TM_E), lambda i: (_kv_block(i), 0, 0)),
            pl.BlockSpec((TM_E, 2 * D_FOURIER), lambda i: (i, 0)),
        ],
        out_shape=[
            jax.ShapeDtypeStruct((D_Q, N_TOK), BF16),
            jax.ShapeDtypeStruct((BATCH * K_ROWS, D_KV), BF16),
            jax.ShapeDtypeStruct((BATCH * KV_BLOCKS, N_KV_HEADS * V_ROWS, TM_E), BF16),
            jax.ShapeDtypeStruct((N_TOK, 2 * D_FOURIER), F32),
        ],
        scratch_shapes=[pltpu.VMEM((D_KV, TM_E), F32),
                        pltpu.VMEM((D_MODEL // LANES, d_qkv, LANES), F32),
                        pltpu.VMEM((d_qkv, D_MODEL), BF16),
                        pltpu.VMEM((D_MODEL, D_FOURIER), BF16)],
        compiler_params=_params(48),
        name="proj_even",
    )(*toks, mod, g, w_in, gain, _ROPE_COS_T, _ROPE_SIN_T, *_CC, *_SC)


def _attn_kernel(par_ref, qt_ref, k_ref, vt_ref, o_ref, ot_scr):
    qi = pl.program_id(1)
    is_latent = qi < NQ
    use_bound = par_ref[0] > 0.5
    k_norm_bound = par_ref[1]
    blocks_per_chunk = CH // TM_E

    def group_consts(g):
        pair, side = divmod(g, 2)
        lanes = slice(LANES * pair, LANES * (pair + 1))
        rows = slice(V_ROWS * g, V_ROWS * (g + 1))
        qpads, qnorms = [], []
        for hh in range(GQA_GROUP):
            row0 = (GQA_GROUP * g + hh) * HEAD_DIM
            q_h = qt_ref[row0:row0 + HEAD_DIM, :]
            zero = jnp.zeros_like(q_h)
            qpads.append(jnp.concatenate([q_h, zero] if side == 0 else [zero, q_h], axis=0))
            qf = q_h.astype(F32)
            qnorms.append(jnp.sqrt(jnp.sum(qf * qf, axis=0, keepdims=True)))
        return lanes, rows, qpads, qnorms

    def pv(vt_blocks, p, acc):
        for j, vt in enumerate(vt_blocks):
            acc = acc + jnp.dot(vt, p[TM_E * j:TM_E * (j + 1), :], preferred_element_type=F32)
        return acc

    def store_heads(g, accs):
        for hh, acc in enumerate(accs):
            row0 = (GQA_GROUP * g + hh) * HEAD_DIM
            ot_scr[row0:row0 + HEAD_DIM, :] = acc[0:HEAD_DIM] / acc[HEAD_DIM:HEAD_DIM + 1]

    def online_step(k_chunk, vt_blocks, qpads, carries):
        sts = [jnp.dot(k_chunk, qpad, preferred_element_type=F32) for qpad in qpads]
        mid = []
        for st, (m, acc) in zip(sts, carries):
            m_new = jnp.maximum(m, jnp.max(st, axis=0, keepdims=True))
            mid.append((m_new, jnp.exp2(m - m_new), jnp.exp2(st - m_new).astype(BF16)))
        return tuple((m_new, pv(vt_blocks, p, alpha * acc))
                     for (m_new, alpha, p), (_, acc) in zip(mid, carries))

    def chunk_operands(c, lanes, rows):
        k0 = pl.multiple_of(c * CH, CH)
        return (k_ref[pl.ds(k0, CH), lanes],
                [vt_ref[c * blocks_per_chunk + j, rows, :] for j in range(blocks_per_chunk)])

    zero_acc = jnp.zeros((V_ROWS, TQ), F32)
    online_init = (jnp.full((1, TQ), -jnp.inf, F32), zero_acc)

    @pl.when(jnp.logical_and(is_latent, use_bound))
    def _():
        for g in range(N_KV_HEADS):
            lanes, rows, qpads, qnorms = group_consts(g)
            bounds = [qn * k_norm_bound for qn in qnorms]

            def body(c, accs, lanes=lanes, rows=rows, qpads=qpads, bounds=bounds):
                k_chunk, vt_blocks = chunk_operands(c, lanes, rows)
                sts = [jnp.dot(k_chunk, qpad, preferred_element_type=F32) for qpad in qpads]
                ps = [jnp.exp2(st - u).astype(BF16) for st, u in zip(sts, bounds)]
                return tuple(pv(vt_blocks, p, acc) for p, acc in zip(ps, accs))

            accs = lax.fori_loop(0, K_ROWS // CH, body, (zero_acc,) * GQA_GROUP)
            store_heads(g, accs)

    @pl.when(jnp.logical_and(is_latent, jnp.logical_not(use_bound)))
    def _():
        for g in range(N_KV_HEADS):
            lanes, rows, qpads, _ = group_consts(g)

            def body(c, carries, lanes=lanes, rows=rows, qpads=qpads):
                k_chunk, vt_blocks = chunk_operands(c, lanes, rows)
                return online_step(k_chunk, vt_blocks, qpads, carries)

            carries = lax.fori_loop(0, K_ROWS // CH, body, (online_init,) * GQA_GROUP)
            store_heads(g, [acc for _, acc in carries])

    @pl.when(jnp.logical_not(is_latent))
    def _():
        for g in range(N_KV_HEADS):
            lanes, rows, qpads, _ = group_consts(g)
            carries = online_step(k_ref[SEQ:K_ROWS, lanes], [vt_ref[KV_BLOCKS - 1, rows, :]], qpads,
                                  (online_init,) * GQA_GROUP)
            store_heads(g, [acc for _, acc in carries])

    o_ref[...] = ot_scr[...].T.astype(BF16)


def _attn_call(par, qt, k, vt3):
    def q_block(b, qi):
        return jnp.where(qi < NQ, b * NQ + qi, N_X // TQ + b)

    return pl.pallas_call(
        _attn_kernel,
        grid=(BATCH, NQ + 1),
        in_specs=[
            pl.BlockSpec(memory_space=pltpu.SMEM),
            pl.BlockSpec((D_Q, TQ), lambda b, qi: (0, q_block(b, qi))),
            pl.BlockSpec((K_ROWS, D_KV), lambda b, qi: (b, 0)),
            pl.BlockSpec((KV_BLOCKS, N_KV_HEADS * V_ROWS, TM_E), lambda b, qi: (b, 0, 0)),
        ],
        out_specs=pl.BlockSpec((TQ, D_Q), lambda b, qi: (q_block(b, qi), 0)),
        out_shape=jax.ShapeDtypeStruct((N_TOK, D_Q), BF16),
        scratch_shapes=[pltpu.VMEM((D_Q, TQ), F32)],
        compiler_params=_params(48),
        name="attn",
    )(par, qt, k, vt3)


def _attn_params(q_gain, k_gain):
    slack = 1.0 + 2.0 ** -6
    k_norm_bound = (HEAD_DIM ** 0.5) * jnp.max(jnp.abs(k_gain)) * slack
    q_norm_bound = (HEAD_DIM ** 0.5) * jnp.max(jnp.abs(q_gain)) * Q_SCALE * slack
    use_bound = (q_norm_bound * k_norm_bound < MAX_SCORE_BOUND).astype(F32)
    return jnp.stack([use_bound, k_norm_bound]).astype(F32)


def _fourier1_kernel(w_ref, m_ref, z_ref):
    for j in range(TA):
        xa = w_ref[:, j, :]
        y = jnp.dot(m_ref[j], xa, precision=HIGHEST, preferred_element_type=F32)
        z_ref[j, 0:FFT_B, :] = y[0:FFT_B, 0:D_FOURIER] + y[FFT_B:, D_FOURIER:]
        z_ref[j, FFT_B:, :] = y[0:FFT_B, D_FOURIER:] - y[FFT_B:, 0:D_FOURIER]


def _fourier1_call(w):
    w3 = w.reshape(N_TOK // FFT_A, FFT_A, 2 * D_FOURIER)
    return pl.pallas_call(
        _fourier1_kernel,
        grid=(BATCH, FFT_A // TA),
        in_specs=[
            pl.BlockSpec((FFT_B, TA, 2 * D_FOURIER), lambda b, a: (b, a, 0)),
            pl.BlockSpec((TA, 2 * FFT_B, FFT_B), lambda b, a: (a, 0, 0)),
        ],
        out_specs=pl.BlockSpec((None, TA, 2 * FFT_B, D_FOURIER), lambda b, a: (b, a, 0, 0)),
        out_shape=jax.ShapeDtypeStruct((BATCH, FFT_A, 2 * FFT_B, D_FOURIER), F32),
        compiler_params=_params(32),
        name="fourier1",
    )(w3, _STAGE1)


def _fourier2_kernel(zr_ref, zi_ref, ch_ref, cl_ref, sh_ref, sl_ref, o_ref):
    zr = jnp.concatenate([zr_ref[:, j, :] for j in range(TB2)], axis=1)
    zi = jnp.concatenate([zi_ref[:, j, :] for j in range(TB2)], axis=1)
    res = (_dot3((ch_ref[...], cl_ref[...]), _split(zr))
           + _dot3((sh_ref[...], sl_ref[...]), _split(zi))) * FOURIER_SCALE_X
    for j in range(TB2):
        o_ref[:, j, :] = res[:, D_FOURIER * j:D_FOURIER * (j + 1)]


def _fourier2_call(z):
    n_j = FFT_B // TB2
    out = pl.pallas_call(
        _fourier2_kernel,
        grid=(BATCH, n_j),
        in_specs=[
            pl.BlockSpec((None, FFT_A, TB2, D_FOURIER), lambda b, j: (b, 0, j, 0)),
            pl.BlockSpec((None, FFT_A, TB2, D_FOURIER), lambda b, j: (b, 0, j + n_j, 0)),
        ] + [_const_spec((FFT_A, FFT_A))] * 4,
        out_specs=pl.BlockSpec((None, FFT_A, TB2, D_FOURIER), lambda b, j: (b, 0, j, 0)),
        out_shape=jax.ShapeDtypeStruct((BATCH, FFT_A, FFT_B, D_FOURIER), F32),
        compiler_params=_params(32),
        name="fourier2",
    )(z, z, *_C128, *_S128)
    return out.reshape(N_X, D_FOURIER)


def _fourier_ctx_kernel(w_ref, ch_ref, cl_ref, sh_ref, sl_ref, o_ref):
    w = w_ref[...]
    o_ref[...] = (_dot3((ch_ref[...], cl_ref[...]), _split(w[:, 0:D_FOURIER]))
                  + _dot3((sh_ref[...], sl_ref[...]), _split(w[:, D_FOURIER:]))) * FOURIER_SCALE_C


def _fourier_ctx_call(w):
    return pl.pallas_call(
        _fourier_ctx_kernel,
        grid=(BATCH,),
        in_specs=[
            pl.BlockSpec((CTX_LEN, 2 * D_FOURIER), lambda b: (N_X // CTX_LEN + b, 0)),
        ] + [_const_spec((CTX_LEN, CTX_LEN))] * 4,
        out_specs=pl.BlockSpec((CTX_LEN, D_FOURIER), lambda b: (b, 0)),
        out_shape=jax.ShapeDtypeStruct((N_C, D_FOURIER), F32),
        name="fourier_ctx",
    )(w, *_C256, *_S256)


def _proj_odd_kernel(x_ref, mod_ref, g_ref, w_ref, b_ref, u_ref):
    h = _norm_mod(x_ref[...], g_ref[...], mod_ref[0:1, :], mod_ref[1:2, :]).astype(BF16)
    a = jnp.dot(h, w_ref[...], preferred_element_type=F32) + b_ref[...]
    u_ref[...] = a[:, 0:D_MODEL] * jax.nn.sigmoid(a[:, D_MODEL:])


def _proj_odd_call(tok, mod, g, w, b, n_blocks):
    return pl.pallas_call(
        _proj_odd_kernel,
        grid=(n_blocks,),
        in_specs=[
            pl.BlockSpec((TM, D_MODEL), lambda i: (i, 0)),
            pl.BlockSpec((None, N_MOD, D_MODEL), lambda i: (_mod_block(i), 0, 0)),
            _const_spec((1, D_MODEL)),
            _const_spec((D_MODEL, 2 * D_MODEL)),
            _const_spec((1, 2 * D_MODEL)),
        ],
        out_specs=pl.BlockSpec((TM, D_MODEL), lambda i: (i, 0)),
        out_shape=jax.ShapeDtypeStruct((n_blocks * TM, D_MODEL), F32),
        compiler_params=_params(48),
        name="proj_odd",
    )(tok, mod, g, w, b)


def _dwconv_kernel(prev_ref, cur_ref, next_ref, w_ref, b_ref, lng_ref, lnb_ref, o_ref,
                   ext_scr, conv_scr):
    i = pl.program_id(0)
    blocks_per_seq = SEQ // CONV_TM
    is_ctx = i >= N_X // CONV_TM
    first = jnp.logical_or(is_ctx, i % blocks_per_seq == 0)
    last = jnp.logical_or(is_ctx, i % blocks_per_seq == blocks_per_seq - 1)
    ext_scr[0:CONV_HALO, :] = jnp.where(first, 0.0, prev_ref[...])
    ext_scr[CONV_HALO:CONV_HALO + CONV_TM, :] = cur_ref[...]
    ext_scr[CONV_HALO + CONV_TM:, :] = jnp.where(last, 0.0, next_ref[...])
    off = CONV_HALO - PAD
    win = CONV_RC + SUBLANES
    n_m = (CONV_WIDTH - 1 + off) // SUBLANES + 1
    for lc in range(D_MODEL // LANES):
        lanes = slice(LANES * lc, LANES * (lc + 1))
        for base in range(0, CONV_TM, CONV_RC):
            acc = jnp.broadcast_to(b_ref[:, lanes], (CONV_RC, LANES))
            for r in range(SUBLANES):
                q = None
                for m in range(n_m):
                    k = SUBLANES * m + r - off
                    if 0 <= k < CONV_WIDTH:
                        lo = base + SUBLANES * m
                        term = w_ref[k:k + 1, lanes] * ext_scr[lo:lo + win, lanes]
                        q = term if q is None else q + term
                acc = acc + q[r:r + CONV_RC]
            conv_scr[base:base + CONV_RC, lanes] = acc
    u = conv_scr[...]
    mu = jnp.mean(u, axis=-1, keepdims=True)
    d = u - mu
    var = jnp.mean(d * d, axis=-1, keepdims=True)
    y = d * lax.rsqrt(var + EPS) * lng_ref[...] + lnb_ref[...]
    o_ref[...] = (y * jax.nn.sigmoid(y)).astype(BF16)


def _dwconv_call(u, w_dw, b_dw, ln_g, ln_b):
    n_rows = u.shape[0]
    n_blocks = n_rows // CONV_TM
    halo_per_block = CONV_TM // CONV_HALO
    n_halo = n_rows // CONV_HALO
    return pl.pallas_call(
        _dwconv_kernel,
        grid=(n_blocks,),
        in_specs=[
            pl.BlockSpec((CONV_HALO, D_MODEL), lambda i: (jnp.maximum(i * halo_per_block - 1, 0), 0)),
            pl.BlockSpec((CONV_TM, D_MODEL), lambda i: (i, 0)),
            pl.BlockSpec((CONV_HALO, D_MODEL),
                         lambda i: (jnp.minimum((i + 1) * halo_per_block, n_halo - 1), 0)),
            _const_spec((CONV_WIDTH, D_MODEL)),
            _const_spec((1, D_MODEL)),
            _const_spec((1, D_MODEL)),
            _const_spec((1, D_MODEL)),
        ],
        out_specs=pl.BlockSpec((CONV_TM, D_MODEL), lambda i: (i, 0)),
        out_shape=jax.ShapeDtypeStruct((n_rows, D_MODEL), BF16),
        scratch_shapes=[pltpu.VMEM((CONV_TM + 2 * CONV_HALO, D_MODEL), F32),
                        pltpu.VMEM((CONV_TM, D_MODEL), F32)],
        compiler_params=_params(32),
        name="dwconv",
    )(u, u, u, w_dw, b_dw, ln_g, ln_b)


def _mlp_tail(x1, mod_ref, g2_ref, w1_ref, w2_ref, out_ref):
    h2 = _norm_mod(x1, g2_ref[...], mod_ref[3:4, :], mod_ref[4:5, :]).astype(BF16)
    acc = jnp.zeros((TM, D_MODEL), F32)
    for ck in range(D_FF // FF_CHUNK):
        cols = slice(FF_CHUNK * ck, FF_CHUNK * (ck + 1))
        a = jnp.maximum(jnp.dot(h2, w1_ref[:, cols], preferred_element_type=F32), 0.0)
        acc = acc + jnp.dot((a * a).astype(BF16), w2_ref[cols, :], preferred_element_type=F32)
    out_ref[...] = x1 + mod_ref[5:6, :] * acc


def _post_even_kernel(*refs, n_tok_refs):
    (mod_ref, o_ref, fx_ref, fc_ref, wo_ref, g2_ref, w1_ref, w2_ref, out_ref) = refs[n_tok_refs:]
    i = pl.program_id(0)
    fo = jnp.where(i >= X_BLOCKS, fc_ref[...], fx_ref[...]).astype(BF16)
    y = (jnp.dot(o_ref[...], wo_ref[0:D_Q, :], preferred_element_type=F32)
         + jnp.dot(fo, wo_ref[D_Q:, :], preferred_element_type=F32))
    x1 = _token_block(refs, n_tok_refs, X_BLOCKS) + mod_ref[2:3, :] * y
    _mlp_tail(x1, mod_ref, g2_ref, w1_ref, w2_ref, out_ref)


def _post_odd_kernel(x_ref, mod_ref, v_ref, wo_ref, bo_ref, g2_ref, w1_ref, w2_ref, out_ref):
    y = jnp.dot(v_ref[...], wo_ref[...], preferred_element_type=F32) + bo_ref[...]
    x1 = x_ref[...] + mod_ref[2:3, :] * y
    _mlp_tail(x1, mod_ref, g2_ref, w1_ref, w2_ref, out_ref)


def _resident(shape):
    nd = len(shape)
    return pl.BlockSpec(shape, lambda *_: (0,) * nd, pipeline_mode=pl.Buffered(1))


def _post_even_call(tok, mod, o, f_x, f_c, wo, g2, w1, w2):
    toks, tok_specs = _token_specs(tok, TM)
    return pl.pallas_call(
        functools.partial(_post_even_kernel, n_tok_refs=len(toks)),
        grid=(TOK_BLOCKS,),
        in_specs=tok_specs + [
            pl.BlockSpec((None, N_MOD, D_MODEL), lambda i: (_mod_block(i), 0, 0)),
            pl.BlockSpec((TM, D_Q), lambda i: (i, 0)),
            pl.BlockSpec((TM, D_FOURIER), lambda i: (jnp.minimum(i, X_BLOCKS - 1), 0)),
            _const_spec((N_C, D_FOURIER)),
            _resident((D_Q + D_FOURIER, D_MODEL)),
            _const_spec((1, D_MODEL)),
            _resident((D_MODEL, D_FF)),
            _resident((D_FF, D_MODEL)),
        ],
        out_specs=pl.BlockSpec((TM, D_MODEL), lambda i: (i, 0)),
        out_shape=jax.ShapeDtypeStruct((N_TOK, D_MODEL), F32),
        compiler_params=_params(56),
        name="post_even",
    )(*toks, mod, o, f_x, f_c, wo, g2, w1, w2)


def _post_odd_call(tok, mod, v, wo, bo, g2, w1, w2, n_blocks):
    return pl.pallas_call(
        _post_odd_kernel,
        grid=(n_blocks,),
        in_specs=[
            pl.BlockSpec((TM, D_MODEL), lambda i: (i, 0)),
            pl.BlockSpec((None, N_MOD, D_MODEL), lambda i: (_mod_block(i), 0, 0)),
            pl.BlockSpec((TM, D_MODEL), lambda i: (i, 0)),
            _resident((D_MODEL, D_MODEL)),
            _const_spec((1, D_MODEL)),
            _const_spec((1, D_MODEL)),
            _resident((D_MODEL, D_FF)),
            _resident((D_FF, D_MODEL)),
        ],
        out_specs=pl.BlockSpec((TM, D_MODEL), lambda i: (i, 0)),
        out_shape=jax.ShapeDtypeStruct((n_blocks * TM, D_MODEL), F32),
        compiler_params=_params(56),
        name="post_odd",
    )(tok, mod, v, wo, bo, g2, w1, w2)


def _split_halves(w_cols):
    lead = w_cols.shape[:-1]
    n_heads = w_cols.shape[-1] // HEAD_DIM
    w4 = w_cols.reshape(*lead, n_heads, HALF, 2)
    return jnp.swapaxes(w4, -1, -2).reshape(*lead, n_heads * HEAD_DIM)


def kernel(x, c, ctx, c_ctx, ada_w, ada_b, norm1_g, norm2_g, mlp_w1, mlp_w2, attn_w_in, q_norm_g, k_norm_g, attn_w_out, conv_w_pw1, conv_b_pw1, conv_w_dw, conv_b_dw, conv_ln_g, conv_ln_b, conv_w_pw2, conv_b_pw2):
    tok = (x.reshape(N_X, D_MODEL), ctx.reshape(N_C, D_MODEL))
    c8 = jnp.concatenate([c, c_ctx[None, :], jnp.zeros((SUBLANES - BATCH - 1, D_MODEL), F32)], axis=0)
    mods = _ada_call(c8, ada_w, ada_b)[:, :BATCH + 1].reshape(DEPTH, BATCH + 1, N_MOD, D_MODEL)

    for i in range(DEPTH):
        j = i // 2
        last = i == DEPTH - 1
        n_blocks = X_BLOCKS if last else TOK_BLOCKS
        mod = mods[i]
        g1 = norm1_g[i][None, :]
        g2 = norm2_g[i][None, :]
        w1 = mlp_w1[i].astype(BF16)
        w2 = mlp_w2[i].astype(BF16)
        if i % 2 == 0:
            gain = jnp.concatenate([jnp.tile(_split_halves(q_norm_g[j]), N_Q_HEADS),
                                    jnp.tile(_split_halves(k_norm_g[j]), N_KV_HEADS)])[:, None]
            qt, k, vt3, w = _proj_even_call(tok, mod, g1, attn_w_in[j], gain)
            o = _attn_call(_attn_params(q_norm_g[j], k_norm_g[j]), qt, k, vt3)
            f_x = _fourier2_call(_fourier1_call(w))
            f_c = _fourier_ctx_call(w)
            tok = _post_even_call(tok, mod, o, f_x, f_c, attn_w_out[j].astype(BF16), g2, w1, w2)
        else:
            u = _proj_odd_call(tok, mod, g1, conv_w_pw1[j].astype(BF16), conv_b_pw1[j][None, :], n_blocks)
            v = _dwconv_call(u, conv_w_dw[j], conv_b_dw[j][None, :], conv_ln_g[j][None, :],
                             conv_ln_b[j][None, :])
            tok = _post_odd_call(tok, mod, v, conv_w_pw2[j].astype(BF16), conv_b_pw2[j][None, :],
                                 g2, w1, w2, n_blocks)
    return tok[:N_X].reshape(BATCH, SEQ, D_MODEL)
```

```python
import functools

import ml_dtypes
import numpy as np
import jax
import jax.numpy as jnp
from jax import lax
from jax.experimental import pallas as pl
from jax.experimental.pallas import tpu as pltpu

D_MODEL = 1024
BATCH = 2
SEQ = 8192
DEPTH = 4
GRID_W = 64
CTX_LEN = 256
HEAD_DIM = 64
N_Q_HEADS = 12
N_KV_HEADS = 4
GQA_GROUP = N_Q_HEADS // N_KV_HEADS
D_Q = N_Q_HEADS * HEAD_DIM
D_KV = N_KV_HEADS * HEAD_DIM
D_FOURIER = 256
FOURIER_GROUP_DIM = 64
ROPE_PAIRS_PER_AXIS = HEAD_DIM // 4
ROPE_THETA = 10000.0
CONV_WIDTH = 31
D_FF = 4 * D_MODEL
N_MOD = 6
EPS = 1e-6
ATTN_SCALE = HEAD_DIM ** -0.5
LOG2_E = 1.4426950408889634
Q_SCALE = ATTN_SCALE * LOG2_E

N_X = BATCH * SEQ
N_C = BATCH * CTX_LEN
N_TOK = N_X + N_C
N_HEADS_QK = N_Q_HEADS + N_KV_HEADS
HALF = HEAD_DIM // 2

LANES = 128
SUBLANES = 8
MIB = 1024 * 1024

TM = 512
X_BLOCKS = N_X // TM
TOK_BLOCKS = N_TOK // TM
TM_E = CTX_LEN
TQ = CTX_LEN
NQ = SEQ // TQ
K_ROWS = SEQ + CTX_LEN
KV_BLOCKS = K_ROWS // TM_E
CH = 11 * TM_E
V_ROWS = HEAD_DIM + 16
MAX_SCORE_BOUND = 50.0
FF_CHUNK = 512
TN_ADA = 1536
FFT_A = 128
FFT_B = SEQ // FFT_A
TA = 8
TB2 = 8
CONV_TM = CTX_LEN
CONV_HALO = 16
CONV_RC = 64
PAD = CONV_WIDTH // 2

F32 = jnp.float32
BF16 = jnp.bfloat16
HIGHEST = lax.Precision.HIGHEST


def _rope_tables_t():
    freqs = (np.float32(ROPE_THETA) ** (-np.arange(ROPE_PAIRS_PER_AXIS, dtype=np.float32)
                                        / np.float32(ROPE_PAIRS_PER_AXIS))).astype(np.float32)
    t = np.arange(SEQ)
    row = (t // GRID_W).astype(np.float32)
    col = (t % GRID_W).astype(np.float32)
    ang = np.concatenate([row[:, None] * freqs, col[:, None] * freqs], axis=-1).astype(np.float32)
    cos = np.cos(ang.astype(np.float64)).T
    sin = np.sin(ang.astype(np.float64)).T
    cos = np.concatenate([cos, cos, np.ones((HALF, N_C))], axis=1)
    sin = np.concatenate([sin, sin, np.zeros((HALF, N_C))], axis=1)
    return cos.astype(np.float32), sin.astype(np.float32)


def _dft_cos_sin(n_out, n_in, period):
    k = np.arange(n_out)[:, None].astype(np.int64)
    n = np.arange(n_in)[None, :].astype(np.int64)
    ang = 2.0 * np.pi * ((k * n) % period).astype(np.float64) / period
    return np.cos(ang), np.sin(ang)


def _channel_dft_tables():
    c, s = _dft_cos_sin(FOURIER_GROUP_DIM, FOURIER_GROUP_DIM, FOURIER_GROUP_DIM)
    n_groups = D_FOURIER // FOURIER_GROUP_DIM
    cc = np.kron(np.eye(n_groups), c)
    sc = np.kron(np.eye(n_groups), s)
    return cc.astype(np.float32), sc.astype(np.float32)


def _stage1_tables():
    bp = np.arange(FFT_B)[None, :, None].astype(np.int64)
    a = np.arange(FFT_A)[:, None, None].astype(np.int64)
    b = np.arange(FFT_B)[None, None, :].astype(np.int64)
    ang = 2.0 * np.pi * ((bp * (a + FFT_A * b)) % SEQ).astype(np.float64) / SEQ
    return np.concatenate([np.cos(ang), np.sin(ang)], axis=1).astype(np.float32)


def _hi_lo(table):
    t32 = np.asarray(table, np.float32)
    hi = t32.astype(ml_dtypes.bfloat16)
    lo = (t32 - hi.astype(np.float32)).astype(ml_dtypes.bfloat16)
    return hi, lo


_ROPE_COS_T, _ROPE_SIN_T = _rope_tables_t()
_CC, _SC = (_hi_lo(t) for t in _channel_dft_tables())
_STAGE1 = _stage1_tables()
_C128, _S128 = (_hi_lo(t) for t in _dft_cos_sin(FFT_A, FFT_A, FFT_A))
_C256, _S256 = (_hi_lo(t) for t in _dft_cos_sin(CTX_LEN, CTX_LEN, CTX_LEN))
FOURIER_SCALE_X = float((SEQ * FOURIER_GROUP_DIM) ** -0.5)
FOURIER_SCALE_C = float((CTX_LEN * FOURIER_GROUP_DIM) ** -0.5)


def _mod_block(i, tm=TM):
    return jnp.minimum(i // (SEQ // tm), BATCH)


def _const_spec(shape):
    nd = len(shape)
    return pl.BlockSpec(shape, lambda *_: (0,) * nd)


def _norm_mod(x, g, shift, scale):
    ms = jnp.mean(x * x, axis=-1, keepdims=True)
    return (x * lax.rsqrt(ms + EPS) * g) * (1.0 + scale) + shift


def _split(x):
    hi = x.astype(BF16)
    return hi, (x - hi.astype(F32)).astype(BF16)


def _dot3(a, b):
    def d(p, q):
        return jnp.dot(p, q, preferred_element_type=F32)
    return d(a[0], b[0]) + (d(a[1], b[0]) + d(a[0], b[1]))


def _params(vmem_mib):
    return pltpu.CompilerParams(vmem_limit_bytes=vmem_mib * MIB)


def _ada_kernel(c_ref, w_ref, b_ref, o_ref):
    c = c_ref[...]
    s = c * jax.nn.sigmoid(c)
    o_ref[...] = jnp.dot(s.astype(BF16), w_ref[...].astype(BF16),
                         preferred_element_type=F32) + b_ref[...]


def _ada_call(c8, ada_w, ada_b):
    n_out = N_MOD * D_MODEL
    return pl.pallas_call(
        _ada_kernel,
        grid=(DEPTH, n_out // TN_ADA),
        in_specs=[
            _const_spec((SUBLANES, D_MODEL)),
            pl.BlockSpec((None, D_MODEL, TN_ADA), lambda l, j: (l, 0, j)),
            pl.BlockSpec((None, 1, TN_ADA), lambda l, j: (l, 0, j)),
        ],
        out_specs=pl.BlockSpec((None, SUBLANES, TN_ADA), lambda l, j: (l, 0, j)),
        out_shape=jax.ShapeDtypeStruct((DEPTH, SUBLANES, n_out), F32),
        compiler_params=_params(40),
        name="ada",
    )(c8, ada_w, ada_b.reshape(DEPTH, 1, n_out))


def _kv_block(i):
    n_x = N_X // TM_E
    per_batch = SEQ // TM_E
    return jnp.where(i < n_x, (i // per_batch) * KV_BLOCKS + i % per_batch,
                     (i - n_x) * KV_BLOCKS + per_batch)


def _token_block(refs, n_tok_refs, n_latent_blocks):
    if n_tok_refs == 1:
        return refs[0][...]
    return jnp.where(pl.program_id(0) >= n_latent_blocks, refs[1][...], refs[0][...])


def _token_specs(tok, tm):
    if not isinstance(tok, tuple):
        return (tok,), [pl.BlockSpec((tm, D_MODEL), lambda i: (i, 0))]
    n_x = N_X // tm
    return tok, [pl.BlockSpec((tm, D_MODEL), lambda i: (jnp.minimum(i, n_x - 1), 0)),
                 pl.BlockSpec((tm, D_MODEL), lambda i: (jnp.maximum(i - n_x, 0), 0))]


def _proj_even_kernel(*refs, n_tok_refs):
    (mod_ref, g_ref, win_ref, gain_ref, cos_ref, sin_ref, cch_ref, ccl_ref, sch_ref, scl_ref,
     qt_ref, k_ref, vt_ref, w_ref, kt_scr, wt32_scr, wqkvt_scr, wf_scr) = refs[n_tok_refs:]
    d_qk = D_Q + D_KV
    d_qkv = D_Q + 2 * D_KV

    @pl.when(pl.program_id(0) == 0)
    def _():
        wt = win_ref[:, 0:d_qkv].T
        wqkvt_scr[d_qk:d_qkv, :] = wt[d_qk:d_qkv, :].astype(BF16)
        for c in range(D_MODEL // LANES):
            cols = slice(LANES * c, LANES * (c + 1))
            wt32_scr[c] = wt[:, cols]
            for hd in range(N_HEADS_QK):
                lo = HEAD_DIM * hd
                wqkvt_scr[lo:lo + HALF, cols] = wt32_scr[c, pl.ds(lo, HALF, stride=2), :].astype(BF16)
                wqkvt_scr[lo + HALF:lo + HEAD_DIM, cols] = (
                    wt32_scr[c, pl.ds(lo + 1, HALF, stride=2), :].astype(BF16))
        wf_scr[...] = win_ref[:, d_qkv:].astype(BF16)

    x = _token_block(refs, n_tok_refs, N_X // TM_E)
    h = _norm_mod(x, g_ref[...], mod_ref[0:1, :], mod_ref[1:2, :]).astype(BF16)
    t = lax.dot_general(wqkvt_scr[...], h, (((1,), (1,)), ((), ())), preferred_element_type=F32)
    cos = cos_ref[...]
    sin = sin_ref[...]
    for hd in range(N_HEADS_QK):
        lo = HEAD_DIM * hd
        x1 = t[lo:lo + HALF]
        x2 = t[lo + HALF:lo + HEAD_DIM]
        ms = (jnp.sum(x1 * x1, axis=0, keepdims=True)
              + jnp.sum(x2 * x2, axis=0, keepdims=True)) * (1.0 / HEAD_DIM)
        r = lax.rsqrt(ms + EPS)
        x1 = x1 * r * gain_ref[lo:lo + HALF, :]
        x2 = x2 * r * gain_ref[lo + HALF:lo + HEAD_DIM, :]
        o1 = x1 * cos - x2 * sin
        o2 = x1 * sin + x2 * cos
        if hd < N_Q_HEADS:
            qt_ref[lo:lo + HALF, :] = (o1 * Q_SCALE).astype(BF16)
            qt_ref[lo + HALF:lo + HEAD_DIM, :] = (o2 * Q_SCALE).astype(BF16)
        else:
            kt_scr[lo - D_Q:lo - D_Q + HALF, :] = o1
            kt_scr[lo - D_Q + HALF:lo - D_Q + HEAD_DIM, :] = o2
    k_ref[...] = kt_scr[...].T.astype(BF16)
    ones_rows = (lax.broadcasted_iota(jnp.int32, (V_ROWS - HEAD_DIM, TM_E), 0) == 0).astype(BF16)
    for g in range(N_KV_HEADS):
        src = D_Q + D_KV + HEAD_DIM * g
        vt_ref[V_ROWS * g:V_ROWS * g + HEAD_DIM, :] = t[src:src + HEAD_DIM].astype(BF16)
        vt_ref[V_ROWS * g + HEAD_DIM:V_ROWS * (g + 1), :] = ones_rows
    f = _split(jnp.dot(h, wf_scr[...], preferred_element_type=F32))
    w_ref[:, 0:D_FOURIER] = _dot3(f, (cch_ref[...], ccl_ref[...]))
    w_ref[:, D_FOURIER:2 * D_FOURIER] = -_dot3(f, (sch_ref[...], scl_ref[...]))


def _proj_even_call(tok, mod, g, w_in, j, gain):
    d_qkv = D_Q + 2 * D_KV
    toks, tok_specs = _token_specs(tok, TM_E)
    return pl.pallas_call(
        functools.partial(_proj_even_kernel, n_tok_refs=len(toks)),
        grid=(N_TOK // TM_E,),
        in_specs=tok_specs + [
            pl.BlockSpec((None, N_MOD, D_MODEL), lambda i: (_mod_block(i, TM_E), 0, 0)),
            _const_spec((1, D_MODEL)),
            _layer_resident(w_in, j),
            _const_spec((D_Q + D_KV, 1)),
            pl.BlockSpec((HALF, TM_E), lambda i: (0, i)),
            pl.BlockSpec((HALF, TM_E), lambda i: (0, i)),
        ] + [_const_spec((D_FOURIER, D_FOURIER))] * 4,
        out_specs=[
            pl.BlockSpec((D_Q, TM_E), lambda i: (0, i)),
            pl.BlockSpec((TM_E, D_KV), lambda i: (_kv_block(i), 0)),
            pl.BlockSpec((None, N_KV_HEADS * V_ROWS, TM_E), lambda i: (_kv_block(i), 0, 0)),
            pl.BlockSpec((TM_E, 2 * D_FOURIER), lambda i: (i, 0)),
        ],
        out_shape=[
            jax.ShapeDtypeStruct((D_Q, N_TOK), BF16),
            jax.ShapeDtypeStruct((BATCH * K_ROWS, D_KV), BF16),
            jax.ShapeDtypeStruct((BATCH * KV_BLOCKS, N_KV_HEADS * V_ROWS, TM_E), BF16),
            jax.ShapeDtypeStruct((N_TOK, 2 * D_FOURIER), F32),
        ],
        scratch_shapes=[pltpu.VMEM((D_KV, TM_E), F32),
                        pltpu.VMEM((D_MODEL // LANES, d_qkv, LANES), F32),
                        pltpu.VMEM((d_qkv, D_MODEL), BF16),
                        pltpu.VMEM((D_MODEL, D_FOURIER), BF16)],
        compiler_params=_params(48),
        name="proj_even",
    )(*toks, mod, g, w_in, gain, _ROPE_COS_T, _ROPE_SIN_T, *_CC, *_SC)


def _attn_kernel(par_ref, qt_ref, k_ref, vt_ref, o_ref, ot_scr):
    qi = pl.program_id(1)
    is_latent = qi < NQ
    use_bound = par_ref[0] > 0.5
    k_norm_bound = par_ref[1]
    blocks_per_chunk = CH // TM_E

    def group_consts(g):
        pair, side = divmod(g, 2)
        lanes = slice(LANES * pair, LANES * (pair + 1))
        rows = slice(V_ROWS * g, V_ROWS * (g + 1))
        qpads, qnorms = [], []
        for hh in range(GQA_GROUP):
            row0 = (GQA_GROUP * g + hh) * HEAD_DIM
            q_h = qt_ref[row0:row0 + HEAD_DIM, :]
            zero = jnp.zeros_like(q_h)
            qpads.append(jnp.concatenate([q_h, zero] if side == 0 else [zero, q_h], axis=0))
            qf = q_h.astype(F32)
            qnorms.append(jnp.sqrt(jnp.sum(qf * qf, axis=0, keepdims=True)))
        return lanes, rows, qpads, qnorms

    def pv(vt_blocks, p, acc):
        for j, vt in enumerate(vt_blocks):
            acc = acc + jnp.dot(vt, p[TM_E * j:TM_E * (j + 1), :], preferred_element_type=F32)
        return acc

    def store_heads(g, accs):
        for hh, acc in enumerate(accs):
            row0 = (GQA_GROUP * g + hh) * HEAD_DIM
            ot_scr[row0:row0 + HEAD_DIM, :] = acc[0:HEAD_DIM] / acc[HEAD_DIM:HEAD_DIM + 1]

    def online_step(k_chunk, vt_blocks, qpads, carries):
        sts = [jnp.dot(k_chunk, qpad, preferred_element_type=F32) for qpad in qpads]
        mid = []
        for st, (m, acc) in zip(sts, carries):
            m_new = jnp.maximum(m, jnp.max(st, axis=0, keepdims=True))
            mid.append((m_new, jnp.exp2(m - m_new), jnp.exp2(st - m_new).astype(BF16)))
        return tuple((m_new, pv(vt_blocks, p, alpha * acc))
                     for (m_new, alpha, p), (_, acc) in zip(mid, carries))

    def chunk_operands(c, lanes, rows):
        k0 = pl.multiple_of(c * CH, CH)
        return (k_ref[pl.ds(k0, CH), lanes],
                [vt_ref[c * blocks_per_chunk + j, rows, :] for j in range(blocks_per_chunk)])

    zero_acc = jnp.zeros((V_ROWS, TQ), F32)
    online_init = (jnp.full((1, TQ), -jnp.inf, F32), zero_acc)

    @pl.when(jnp.logical_and(is_latent, use_bound))
    def _():
        for g in range(N_KV_HEADS):
            lanes, rows, qpads, qnorms = group_consts(g)
            bounds = [qn * k_norm_bound for qn in qnorms]

            def body(c, accs, lanes=lanes, rows=rows, qpads=qpads, bounds=bounds):
                k_chunk, vt_blocks = chunk_operands(c, lanes, rows)
                sts = [jnp.dot(k_chunk, qpad, preferred_element_type=F32) for qpad in qpads]
                ps = [jnp.exp2(st - u).astype(BF16) for st, u in zip(sts, bounds)]
                return tuple(pv(vt_blocks, p, acc) for p, acc in zip(ps, accs))

            accs = lax.fori_loop(0, K_ROWS // CH, body, (zero_acc,) * GQA_GROUP)
            store_heads(g, accs)

    @pl.when(jnp.logical_and(is_latent, jnp.logical_not(use_bound)))
    def _():
        for g in range(N_KV_HEADS):
            lanes, rows, qpads, _ = group_consts(g)

            def body(c, carries, lanes=lanes, rows=rows, qpads=qpads):
                k_chunk, vt_blocks = chunk_operands(c, lanes, rows)
                return online_step(k_chunk, vt_blocks, qpads, carries)

            carries = lax.fori_loop(0, K_ROWS // CH, body, (online_init,) * GQA_GROUP)
            store_heads(g, [acc for _, acc in carries])

    @pl.when(jnp.logical_not(is_latent))
    def _():
        for g in range(N_KV_HEADS):
            lanes, rows, qpads, _ = group_consts(g)
            carries = online_step(k_ref[SEQ:K_ROWS, lanes], [vt_ref[KV_BLOCKS - 1, rows, :]], qpads,
                                  (online_init,) * GQA_GROUP)
            store_heads(g, [acc for _, acc in carries])

    o_ref[...] = ot_scr[...].T.astype(BF16)


def _attn_call(par, qt, k, vt3):
    def q_block(b, qi):
        return jnp.where(qi < NQ, b * NQ + qi, N_X // TQ + b)

    return pl.pallas_call(
        _attn_kernel,
        grid=(BATCH, NQ + 1),
        in_specs=[
            pl.BlockSpec(memory_space=pltpu.SMEM),
            pl.BlockSpec((D_Q, TQ), lambda b, qi: (0, q_block(b, qi))),
            pl.BlockSpec((K_ROWS, D_KV), lambda b, qi: (b, 0)),
            pl.BlockSpec((KV_BLOCKS, N_KV_HEADS * V_ROWS, TM_E), lambda b, qi: (b, 0, 0)),
        ],
        out_specs=pl.BlockSpec((TQ, D_Q), lambda b, qi: (q_block(b, qi), 0)),
        out_shape=jax.ShapeDtypeStruct((N_TOK, D_Q), BF16),
        scratch_shapes=[pltpu.VMEM((D_Q, TQ), F32)],
        compiler_params=_params(48),
        name="attn",
    )(par, qt, k, vt3)


def _attn_params(q_gain, k_gain):
    slack = 1.0 + 2.0 ** -6
    k_norm_bound = (HEAD_DIM ** 0.5) * jnp.max(jnp.abs(k_gain)) * slack
    q_norm_bound = (HEAD_DIM ** 0.5) * jnp.max(jnp.abs(q_gain)) * Q_SCALE * slack
    use_bound = (q_norm_bound * k_norm_bound < MAX_SCORE_BOUND).astype(F32)
    return jnp.stack([use_bound, k_norm_bound]).astype(F32)


def _fourier1_kernel(w_ref, m_ref, z_ref):
    for j in range(TA):
        xa = w_ref[:, j, :]
        y = jnp.dot(m_ref[j], xa, precision=HIGHEST, preferred_element_type=F32)
        z_ref[j, 0:FFT_B, :] = y[0:FFT_B, 0:D_FOURIER] + y[FFT_B:, D_FOURIER:]
        z_ref[j, FFT_B:, :] = y[0:FFT_B, D_FOURIER:] - y[FFT_B:, 0:D_FOURIER]


def _fourier1_call(w):
    w3 = w.reshape(N_TOK // FFT_A, FFT_A, 2 * D_FOURIER)
    return pl.pallas_call(
        _fourier1_kernel,
        grid=(BATCH, FFT_A // TA),
        in_specs=[
            pl.BlockSpec((FFT_B, TA, 2 * D_FOURIER), lambda b, a: (b, a, 0)),
            pl.BlockSpec((TA, 2 * FFT_B, FFT_B), lambda b, a: (a, 0, 0)),
        ],
        out_specs=pl.BlockSpec((None, TA, 2 * FFT_B, D_FOURIER), lambda b, a: (b, a, 0, 0)),
        out_shape=jax.ShapeDtypeStruct((BATCH, FFT_A, 2 * FFT_B, D_FOURIER), F32),
        compiler_params=_params(32),
        name="fourier1",
    )(w3, _STAGE1)


def _fourier2_kernel(zr_ref, zi_ref, ch_ref, cl_ref, sh_ref, sl_ref, o_ref):
    zr = jnp.concatenate([zr_ref[:, j, :] for j in range(TB2)], axis=1)
    zi = jnp.concatenate([zi_ref[:, j, :] for j in range(TB2)], axis=1)
    res = (_dot3((ch_ref[...], cl_ref[...]), _split(zr))
           + _dot3((sh_ref[...], sl_ref[...]), _split(zi))) * FOURIER_SCALE_X
    for j in range(TB2):
        o_ref[:, j, :] = res[:, D_FOURIER * j:D_FOURIER * (j + 1)]


def _fourier2_call(z):
    n_j = FFT_B // TB2
    out = pl.pallas_call(
        _fourier2_kernel,
        grid=(BATCH, n_j),
        in_specs=[
            pl.BlockSpec((None, FFT_A, TB2, D_FOURIER), lambda b, j: (b, 0, j, 0)),
            pl.BlockSpec((None, FFT_A, TB2, D_FOURIER), lambda b, j: (b, 0, j + n_j, 0)),
        ] + [_const_spec((FFT_A, FFT_A))] * 4,
        out_specs=pl.BlockSpec((None, FFT_A, TB2, D_FOURIER), lambda b, j: (b, 0, j, 0)),
        out_shape=jax.ShapeDtypeStruct((BATCH, FFT_A, FFT_B, D_FOURIER), F32),
        compiler_params=_params(32),
        name="fourier2",
    )(z, z, *_C128, *_S128)
    return out.reshape(N_X, D_FOURIER)


def _fourier_ctx_kernel(w_ref, ch_ref, cl_ref, sh_ref, sl_ref, o_ref):
    w = w_ref[...]
    o_ref[...] = (_dot3((ch_ref[...], cl_ref[...]), _split(w[:, 0:D_FOURIER]))
                  + _dot3((sh_ref[...], sl_ref[...]), _split(w[:, D_FOURIER:]))) * FOURIER_SCALE_C


def _fourier_ctx_call(w):
    return pl.pallas_call(
        _fourier_ctx_kernel,
        grid=(BATCH,),
        in_specs=[
            pl.BlockSpec((CTX_LEN, 2 * D_FOURIER), lambda b: (N_X // CTX_LEN + b, 0)),
        ] + [_const_spec((CTX_LEN, CTX_LEN))] * 4,
        out_specs=pl.BlockSpec((CTX_LEN, D_FOURIER), lambda b: (b, 0)),
        out_shape=jax.ShapeDtypeStruct((N_C, D_FOURIER), F32),
        name="fourier_ctx",
    )(w, *_C256, *_S256)


def _proj_odd_kernel(x_ref, mod_ref, g_ref, w_ref, b_ref, u_ref):
    h = _norm_mod(x_ref[...], g_ref[...], mod_ref[0:1, :], mod_ref[1:2, :]).astype(BF16)
    a = jnp.dot(h, w_ref[...], preferred_element_type=F32) + b_ref[...]
    u_ref[...] = a[:, 0:D_MODEL] * jax.nn.sigmoid(a[:, D_MODEL:])


def _proj_odd_call(tok, mod, g, w, j, b, n_blocks):
    return pl.pallas_call(
        _proj_odd_kernel,
        grid=(n_blocks,),
        in_specs=[
            pl.BlockSpec((TM, D_MODEL), lambda i: (i, 0)),
            pl.BlockSpec((None, N_MOD, D_MODEL), lambda i: (_mod_block(i), 0, 0)),
            _const_spec((1, D_MODEL)),
            _layer_resident(w, j),
            _const_spec((1, 2 * D_MODEL)),
        ],
        out_specs=pl.BlockSpec((TM, D_MODEL), lambda i: (i, 0)),
        out_shape=jax.ShapeDtypeStruct((n_blocks * TM, D_MODEL), F32),
        compiler_params=_params(48),
        name="proj_odd",
    )(tok, mod, g, w, b)


def _dwconv_kernel(prev_ref, cur_ref, next_ref, w_ref, b_ref, lng_ref, lnb_ref, o_ref,
                   ext_scr, conv_scr):
    i = pl.program_id(0)
    blocks_per_seq = SEQ // CONV_TM
    is_ctx = i >= N_X // CONV_TM
    first = jnp.logical_or(is_ctx, i % blocks_per_seq == 0)
    last = jnp.logical_or(is_ctx, i % blocks_per_seq == blocks_per_seq - 1)
    ext_scr[0:CONV_HALO, :] = jnp.where(first, 0.0, prev_ref[...])
    ext_scr[CONV_HALO:CONV_HALO + CONV_TM, :] = cur_ref[...]
    ext_scr[CONV_HALO + CONV_TM:, :] = jnp.where(last, 0.0, next_ref[...])
    off = CONV_HALO - PAD
    win = CONV_RC + SUBLANES
    n_m = (CONV_WIDTH - 1 + off) // SUBLANES + 1
    for lc in range(D_MODEL // LANES):
        lanes = slice(LANES * lc, LANES * (lc + 1))
        for base in range(0, CONV_TM, CONV_RC):
            acc = jnp.broadcast_to(b_ref[:, lanes], (CONV_RC, LANES))
            for r in range(SUBLANES):
                q = None
                for m in range(n_m):
                    k = SUBLANES * m + r - off
                    if 0 <= k < CONV_WIDTH:
                        lo = base + SUBLANES * m
                        term = w_ref[k:k + 1, lanes] * ext_scr[lo:lo + win, lanes]
                        q = term if q is None else q + term
                acc = acc + q[r:r + CONV_RC]
            conv_scr[base:base + CONV_RC, lanes] = acc
    u = conv_scr[...]
    mu = jnp.mean(u, axis=-1, keepdims=True)
    d = u - mu
    var = jnp.mean(d * d, axis=-1, keepdims=True)
    y = d * lax.rsqrt(var + EPS) * lng_ref[...] + lnb_ref[...]
    o_ref[...] = (y * jax.nn.sigmoid(y)).astype(BF16)


def _dwconv_call(u, w_dw, b_dw, ln_g, ln_b):
    n_rows = u.shape[0]
    n_blocks = n_rows // CONV_TM
    halo_per_block = CONV_TM // CONV_HALO
    n_halo = n_rows // CONV_HALO
    return pl.pallas_call(
        _dwconv_kernel,
        grid=(n_blocks,),
        in_specs=[
            pl.BlockSpec((CONV_HALO, D_MODEL), lambda i: (jnp.maximum(i * halo_per_block - 1, 0), 0)),
            pl.BlockSpec((CONV_TM, D_MODEL), lambda i: (i, 0)),
            pl.BlockSpec((CONV_HALO, D_MODEL),
                         lambda i: (jnp.minimum((i + 1) * halo_per_block, n_halo - 1), 0)),
            _const_spec((CONV_WIDTH, D_MODEL)),
            _const_spec((1, D_MODEL)),
            _const_spec((1, D_MODEL)),
            _const_spec((1, D_MODEL)),
        ],
        out_specs=pl.BlockSpec((CONV_TM, D_MODEL), lambda i: (i, 0)),
        out_shape=jax.ShapeDtypeStruct((n_rows, D_MODEL), BF16),
        scratch_shapes=[pltpu.VMEM((CONV_TM + 2 * CONV_HALO, D_MODEL), F32),
                        pltpu.VMEM((CONV_TM, D_MODEL), F32)],
        compiler_params=_params(32),
        name="dwconv",
    )(u, u, u, w_dw, b_dw, ln_g, ln_b)


def _mlp_tail(x1, mod_ref, g2_ref, w1_ref, w2_ref, out_ref):
    h2 = _norm_mod(x1, g2_ref[...], mod_ref[3:4, :], mod_ref[4:5, :]).astype(BF16)
    acc = jnp.zeros((TM, D_MODEL), F32)
    for ck in range(D_FF // FF_CHUNK):
        cols = slice(FF_CHUNK * ck, FF_CHUNK * (ck + 1))
        a = jnp.maximum(jnp.dot(h2, w1_ref[:, cols], preferred_element_type=F32), 0.0)
        acc = acc + jnp.dot((a * a).astype(BF16), w2_ref[cols, :], preferred_element_type=F32)
    out_ref[...] = x1 + mod_ref[5:6, :] * acc


def _post_even_kernel(*refs, n_tok_refs):
    (mod_ref, o_ref, fx_ref, fc_ref, wo_ref, g2_ref, w1_ref, w2_ref, out_ref) = refs[n_tok_refs:]
    i = pl.program_id(0)
    fo = jnp.where(i >= X_BLOCKS, fc_ref[...], fx_ref[...]).astype(BF16)
    y = (jnp.dot(o_ref[...], wo_ref[0:D_Q, :], preferred_element_type=F32)
         + jnp.dot(fo, wo_ref[D_Q:, :], preferred_element_type=F32))
    x1 = _token_block(refs, n_tok_refs, X_BLOCKS) + mod_ref[2:3, :] * y
    _mlp_tail(x1, mod_ref, g2_ref, w1_ref, w2_ref, out_ref)


def _post_odd_kernel(x_ref, mod_ref, v_ref, wo_ref, bo_ref, g2_ref, w1_ref, w2_ref, out_ref):
    y = jnp.dot(v_ref[...], wo_ref[...], preferred_element_type=F32) + bo_ref[...]
    x1 = x_ref[...] + mod_ref[2:3, :] * y
    _mlp_tail(x1, mod_ref, g2_ref, w1_ref, w2_ref, out_ref)


def _resident(shape):
    nd = len(shape)
    return pl.BlockSpec(shape, lambda *_: (0,) * nd, pipeline_mode=pl.Buffered(1))


def _layer_resident(stack, layer):
    shape = stack.shape[1:]
    return pl.BlockSpec((None,) + shape, lambda *_: (layer,) + (0,) * len(shape),
                        pipeline_mode=pl.Buffered(1))


def _post_even_call(tok, mod, o, f_x, f_c, wo, j, g2, w1, w2, layer):
    toks, tok_specs = _token_specs(tok, TM)
    return pl.pallas_call(
        functools.partial(_post_even_kernel, n_tok_refs=len(toks)),
        grid=(TOK_BLOCKS,),
        in_specs=tok_specs + [
            pl.BlockSpec((None, N_MOD, D_MODEL), lambda i: (_mod_block(i), 0, 0)),
            pl.BlockSpec((TM, D_Q), lambda i: (i, 0)),
            pl.BlockSpec((TM, D_FOURIER), lambda i: (jnp.minimum(i, X_BLOCKS - 1), 0)),
            _const_spec((N_C, D_FOURIER)),
            _layer_resident(wo, j),
            _const_spec((1, D_MODEL)),
            _layer_resident(w1, layer),
            _layer_resident(w2, layer),
        ],
        out_specs=pl.BlockSpec((TM, D_MODEL), lambda i: (i, 0)),
        out_shape=jax.ShapeDtypeStruct((N_TOK, D_MODEL), F32),
        compiler_params=_params(56),
        name="post_even",
    )(*toks, mod, o, f_x, f_c, wo, g2, w1, w2)


def _post_odd_call(tok, mod, v, wo, j, bo, g2, w1, w2, layer, n_blocks):
    return pl.pallas_call(
        _post_odd_kernel,
        grid=(n_blocks,),
        in_specs=[
            pl.BlockSpec((TM, D_MODEL), lambda i: (i, 0)),
            pl.BlockSpec((None, N_MOD, D_MODEL), lambda i: (_mod_block(i), 0, 0)),
            pl.BlockSpec((TM, D_MODEL), lambda i: (i, 0)),
            _layer_resident(wo, j),
            _const_spec((1, D_MODEL)),
            _const_spec((1, D_MODEL)),
            _layer_resident(w1, layer),
            _layer_resident(w2, layer),
        ],
        out_specs=pl.BlockSpec((TM, D_MODEL), lambda i: (i, 0)),
        out_shape=jax.ShapeDtypeStruct((n_blocks * TM, D_MODEL), F32),
        compiler_params=_params(56),
        name="post_odd",
    )(tok, mod, v, wo, bo, g2, w1, w2)


def _split_halves(w_cols):
    lead = w_cols.shape[:-1]
    n_heads = w_cols.shape[-1] // HEAD_DIM
    w4 = w_cols.reshape(*lead, n_heads, HALF, 2)
    return jnp.swapaxes(w4, -1, -2).reshape(*lead, n_heads * HEAD_DIM)


def kernel(x, c, ctx, c_ctx, ada_w, ada_b, norm1_g, norm2_g, mlp_w1, mlp_w2, attn_w_in, q_norm_g, k_norm_g, attn_w_out, conv_w_pw1, conv_b_pw1, conv_w_dw, conv_b_dw, conv_ln_g, conv_ln_b, conv_w_pw2, conv_b_pw2):
    tok = (x.reshape(N_X, D_MODEL), ctx.reshape(N_C, D_MODEL))
    c8 = jnp.concatenate([c, c_ctx[None, :], jnp.zeros((SUBLANES - BATCH - 1, D_MODEL), F32)], axis=0)
    mods = _ada_call(c8, ada_w, ada_b)[:, :BATCH + 1].reshape(DEPTH, BATCH + 1, N_MOD, D_MODEL)

    w1 = mlp_w1.astype(BF16)
    w2 = mlp_w2.astype(BF16)
    w_out = attn_w_out.astype(BF16)
    w_pw1 = conv_w_pw1.astype(BF16)
    w_pw2 = conv_w_pw2.astype(BF16)

    for i in range(DEPTH):
        j = i // 2
        last = i == DEPTH - 1
        n_blocks = X_BLOCKS if last else TOK_BLOCKS
        mod = mods[i]
        g1 = norm1_g[i][None, :]
        g2 = norm2_g[i][None, :]
        if i % 2 == 0:
            gain = jnp.concatenate([jnp.tile(_split_halves(q_norm_g[j]), N_Q_HEADS),
                                    jnp.tile(_split_halves(k_norm_g[j]), N_KV_HEADS)])[:, None]
            qt, k, vt3, w = _proj_even_call(tok, mod, g1, attn_w_in, j, gain)
            o = _attn_call(_attn_params(q_norm_g[j], k_norm_g[j]), qt, k, vt3)
            f_x = _fourier2_call(_fourier1_call(w))
            f_c = _fourier_ctx_call(w)
            tok = _post_even_call(tok, mod, o, f_x, f_c, w_out, j, g2, w1, w2, i)
        else:
            u = _proj_odd_call(tok, mod, g1, w_pw1, j, conv_b_pw1[j][None, :], n_blocks)
            v = _dwconv_call(u, conv_w_dw[j], conv_b_dw[j][None, :], conv_ln_g[j][None, :],
                             conv_ln_b[j][None, :])
            tok = _post_odd_call(tok, mod, v, w_pw2, j, conv_b_pw2[j][None, :], g2, w1, w2, i,
                                 n_blocks)
    return tok[:N_X].reshape(BATCH, SEQ, D_MODEL)
```

```python
import functools

import ml_dtypes
import numpy as np
import jax
import jax.numpy as jnp
from jax import lax
from jax.experimental import pallas as pl
from jax.experimental.pallas import tpu as pltpu

D_MODEL = 1024
BATCH = 2
SEQ = 8192
DEPTH = 4
GRID_W = 64
CTX_LEN = 256
HEAD_DIM = 64
N_Q_HEADS = 12
N_KV_HEADS = 4
GQA_GROUP = N_Q_HEADS // N_KV_HEADS
D_Q = N_Q_HEADS * HEAD_DIM
D_KV = N_KV_HEADS * HEAD_DIM
D_FOURIER = 256
FOURIER_GROUP_DIM = 64
ROPE_PAIRS_PER_AXIS = HEAD_DIM // 4
ROPE_THETA = 10000.0
CONV_WIDTH = 31
D_FF = 4 * D_MODEL
N_MOD = 6
EPS = 1e-6
ATTN_SCALE = HEAD_DIM ** -0.5
LOG2_E = 1.4426950408889634
Q_SCALE = ATTN_SCALE * LOG2_E

N_X = BATCH * SEQ
N_C = BATCH * CTX_LEN
N_TOK = N_X + N_C
N_HEADS_QK = N_Q_HEADS + N_KV_HEADS
HALF = HEAD_DIM // 2

LANES = 128
SUBLANES = 8
MIB = 1024 * 1024

TM = 512
X_BLOCKS = N_X // TM
TOK_BLOCKS = N_TOK // TM
TM_E = CTX_LEN
TQ = CTX_LEN
NQ = SEQ // TQ
K_ROWS = SEQ + CTX_LEN
KV_BLOCKS = K_ROWS // TM_E
CH_ONLINE = 11 * TM_E
V_ROWS = HEAD_DIM + 16
MAX_SCORE_BOUND = 50.0
FF_CHUNK = 512
TN_ADA = 1536
FFT_A = 128
FFT_B = SEQ // FFT_A
TA = 8
TB2 = 8
CONV_TM = CTX_LEN
CONV_HALO = 16
CONV_RC = 128
PAD = CONV_WIDTH // 2

F32 = jnp.float32
BF16 = jnp.bfloat16
HIGHEST = lax.Precision.HIGHEST


def _rope_tables_t():
    freqs = (np.float32(ROPE_THETA) ** (-np.arange(ROPE_PAIRS_PER_AXIS, dtype=np.float32)
                                        / np.float32(ROPE_PAIRS_PER_AXIS))).astype(np.float32)
    t = np.arange(SEQ)
    row = (t // GRID_W).astype(np.float32)
    col = (t % GRID_W).astype(np.float32)
    ang = np.concatenate([row[:, None] * freqs, col[:, None] * freqs], axis=-1).astype(np.float32)
    cos = np.cos(ang.astype(np.float64)).T
    sin = np.sin(ang.astype(np.float64)).T
    cos = np.concatenate([cos, cos, np.ones((HALF, N_C))], axis=1)
    sin = np.concatenate([sin, sin, np.zeros((HALF, N_C))], axis=1)
    return cos.astype(np.float32), sin.astype(np.float32)


def _dft_cos_sin(n_out, n_in, period):
    k = np.arange(n_out)[:, None].astype(np.int64)
    n = np.arange(n_in)[None, :].astype(np.int64)
    ang = 2.0 * np.pi * ((k * n) % period).astype(np.float64) / period
    return np.cos(ang), np.sin(ang)


def _channel_dft_tables():
    c, s = _dft_cos_sin(FOURIER_GROUP_DIM, FOURIER_GROUP_DIM, FOURIER_GROUP_DIM)
    n_groups = D_FOURIER // FOURIER_GROUP_DIM
    cc = np.kron(np.eye(n_groups), c)
    sc = np.kron(np.eye(n_groups), s)
    return cc.astype(np.float32), sc.astype(np.float32)


def _stage1_tables():
    bp = np.arange(FFT_B)[None, :, None].astype(np.int64)
    a = np.arange(FFT_A)[:, None, None].astype(np.int64)
    b = np.arange(FFT_B)[None, None, :].astype(np.int64)
    ang = 2.0 * np.pi * ((bp * (a + FFT_A * b)) % SEQ).astype(np.float64) / SEQ
    return np.concatenate([np.cos(ang), np.sin(ang)], axis=1).astype(np.float32)


def _hi_lo(table):
    t32 = np.asarray(table, np.float32)
    hi = t32.astype(ml_dtypes.bfloat16)
    lo = (t32 - hi.astype(np.float32)).astype(ml_dtypes.bfloat16)
    return hi, lo


_ROPE_COS_T, _ROPE_SIN_T = _rope_tables_t()
_CC, _SC = (_hi_lo(t) for t in _channel_dft_tables())
_STAGE1 = _stage1_tables()
_C128, _S128 = (_hi_lo(t) for t in _dft_cos_sin(FFT_A, FFT_A, FFT_A))
_C256, _S256 = (_hi_lo(t) for t in _dft_cos_sin(CTX_LEN, CTX_LEN, CTX_LEN))
FOURIER_SCALE_X = float((SEQ * FOURIER_GROUP_DIM) ** -0.5)
FOURIER_SCALE_C = float((CTX_LEN * FOURIER_GROUP_DIM) ** -0.5)


def _mod_block(i, tm=TM):
    return jnp.minimum(i // (SEQ // tm), BATCH)


def _const_spec(shape):
    nd = len(shape)
    return pl.BlockSpec(shape, lambda *_: (0,) * nd)


def _norm_mod(x, g, shift, scale):
    ms = jnp.mean(x * x, axis=-1, keepdims=True)
    return (x * lax.rsqrt(ms + EPS) * g) * (1.0 + scale) + shift


def _split(x):
    hi = x.astype(BF16)
    return hi, (x - hi.astype(F32)).astype(BF16)


def _dot3(a, b):
    def d(p, q):
        return jnp.dot(p, q, preferred_element_type=F32)
    return d(a[0], b[0]) + (d(a[1], b[0]) + d(a[0], b[1]))


def _params(vmem_mib):
    return pltpu.CompilerParams(vmem_limit_bytes=vmem_mib * MIB)


def _ada_kernel(c_ref, w_ref, b_ref, o_ref):
    c = c_ref[...]
    s = c * jax.nn.sigmoid(c)
    o_ref[...] = jnp.dot(s.astype(BF16), w_ref[...].astype(BF16),
                         preferred_element_type=F32) + b_ref[...]


def _ada_call(c8, ada_w, ada_b):
    n_out = N_MOD * D_MODEL
    return pl.pallas_call(
        _ada_kernel,
        grid=(DEPTH, n_out // TN_ADA),
        in_specs=[
            _const_spec((SUBLANES, D_MODEL)),
            pl.BlockSpec((None, D_MODEL, TN_ADA), lambda l, j: (l, 0, j)),
            pl.BlockSpec((None, 1, TN_ADA), lambda l, j: (l, 0, j)),
        ],
        out_specs=pl.BlockSpec((None, SUBLANES, TN_ADA), lambda l, j: (l, 0, j)),
        out_shape=jax.ShapeDtypeStruct((DEPTH, SUBLANES, n_out), F32),
        compiler_params=_params(40),
        name="ada",
    )(c8, ada_w, ada_b.reshape(DEPTH, 1, n_out))


def _kv_block(i):
    n_x = N_X // TM_E
    per_batch = SEQ // TM_E
    return jnp.where(i < n_x, (i // per_batch) * KV_BLOCKS + i % per_batch,
                     (i - n_x) * KV_BLOCKS + per_batch)


def _token_block(refs, n_tok_refs, n_latent_blocks):
    if n_tok_refs == 1:
        return refs[0][...]
    return jnp.where(pl.program_id(0) >= n_latent_blocks, refs[1][...], refs[0][...])


def _token_specs(tok, tm):
    if not isinstance(tok, tuple):
        return (tok,), [pl.BlockSpec((tm, D_MODEL), lambda i: (i, 0))]
    n_x = N_X // tm
    return tok, [pl.BlockSpec((tm, D_MODEL), lambda i: (jnp.minimum(i, n_x - 1), 0)),
                 pl.BlockSpec((tm, D_MODEL), lambda i: (jnp.maximum(i - n_x, 0), 0))]


def _proj_even_kernel(*refs, n_tok_refs):
    (mod_ref, g_ref, win_ref, gain_ref, cos_ref, sin_ref, cch_ref, ccl_ref, sch_ref, scl_ref,
     qt_ref, k_ref, vt_ref, w_ref, kt_scr, wt32_scr, wqkvt_scr, wf_scr) = refs[n_tok_refs:]
    d_qk = D_Q + D_KV
    d_qkv = D_Q + 2 * D_KV

    @pl.when(pl.program_id(0) == 0)
    def _():
        wt = win_ref[:, 0:d_qkv].T
        wqkvt_scr[d_qk:d_qkv, :] = wt[d_qk:d_qkv, :].astype(BF16)
        for c in range(D_MODEL // LANES):
            cols = slice(LANES * c, LANES * (c + 1))
            wt32_scr[c] = wt[:, cols]
            for hd in range(N_HEADS_QK):
                lo = HEAD_DIM * hd
                wqkvt_scr[lo:lo + HALF, cols] = wt32_scr[c, pl.ds(lo, HALF, stride=2), :].astype(BF16)
                wqkvt_scr[lo + HALF:lo + HEAD_DIM, cols] = (
                    wt32_scr[c, pl.ds(lo + 1, HALF, stride=2), :].astype(BF16))
        wf_scr[...] = win_ref[:, d_qkv:].astype(BF16)

    x = _token_block(refs, n_tok_refs, N_X // TM_E)
    h = _norm_mod(x, g_ref[...], mod_ref[0:1, :], mod_ref[1:2, :]).astype(BF16)
    t = lax.dot_general(wqkvt_scr[...], h, (((1,), (1,)), ((), ())), preferred_element_type=F32)
    cos = cos_ref[...]
    sin = sin_ref[...]
    for hd in range(N_HEADS_QK):
        lo = HEAD_DIM * hd
        x1 = t[lo:lo + HALF]
        x2 = t[lo + HALF:lo + HEAD_DIM]
        ms = (jnp.sum(x1 * x1, axis=0, keepdims=True)
              + jnp.sum(x2 * x2, axis=0, keepdims=True)) * (1.0 / HEAD_DIM)
        r = lax.rsqrt(ms + EPS)
        x1 = x1 * r * gain_ref[lo:lo + HALF, :]
        x2 = x2 * r * gain_ref[lo + HALF:lo + HEAD_DIM, :]
        o1 = x1 * cos - x2 * sin
        o2 = x1 * sin + x2 * cos
        if hd < N_Q_HEADS:
            qt_ref[lo:lo + HALF, :] = (o1 * Q_SCALE).astype(BF16)
            qt_ref[lo + HALF:lo + HEAD_DIM, :] = (o2 * Q_SCALE).astype(BF16)
        else:
            kt_scr[lo - D_Q:lo - D_Q + HALF, :] = o1
            kt_scr[lo - D_Q + HALF:lo - D_Q + HEAD_DIM, :] = o2
    k_ref[...] = kt_scr[...].T.astype(BF16)
    ones_rows = (lax.broadcasted_iota(jnp.int32, (V_ROWS - HEAD_DIM, TM_E), 0) == 0).astype(BF16)
    for g in range(N_KV_HEADS):
        src = D_Q + D_KV + HEAD_DIM * g
        vt_ref[V_ROWS * g:V_ROWS * g + HEAD_DIM, :] = t[src:src + HEAD_DIM].astype(BF16)
        vt_ref[V_ROWS * g + HEAD_DIM:V_ROWS * (g + 1), :] = ones_rows
    f = _split(jnp.dot(h, wf_scr[...], preferred_element_type=F32))
    w_ref[:, 0:D_FOURIER] = _dot3(f, (cch_ref[...], ccl_ref[...]))
    w_ref[:, D_FOURIER:2 * D_FOURIER] = -_dot3(f, (sch_ref[...], scl_ref[...]))


def _proj_even_call(tok, mod, g, w_in, j, gain):
    d_qkv = D_Q + 2 * D_KV
    toks, tok_specs = _token_specs(tok, TM_E)
    return pl.pallas_call(
        functools.partial(_proj_even_kernel, n_tok_refs=len(toks)),
        grid=(N_TOK // TM_E,),
        in_specs=tok_specs + [
            pl.BlockSpec((None, N_MOD, D_MODEL), lambda i: (_mod_block(i, TM_E), 0, 0)),
            _const_spec((1, D_MODEL)),
            _layer_resident(w_in, j),
            _const_spec((D_Q + D_KV, 1)),
            pl.BlockSpec((HALF, TM_E), lambda i: (0, i)),
            pl.BlockSpec((HALF, TM_E), lambda i: (0, i)),
        ] + [_const_spec((D_FOURIER, D_FOURIER))] * 4,
        out_specs=[
            pl.BlockSpec((D_Q, TM_E), lambda i: (0, i)),
            pl.BlockSpec((TM_E, D_KV), lambda i: (_kv_block(i), 0)),
            pl.BlockSpec((None, N_KV_HEADS * V_ROWS, TM_E), lambda i: (_kv_block(i), 0, 0)),
            pl.BlockSpec((TM_E, 2 * D_FOURIER), lambda i: (i, 0)),
        ],
        out_shape=[
            jax.ShapeDtypeStruct((D_Q, N_TOK), BF16),
            jax.ShapeDtypeStruct((BATCH * K_ROWS, D_KV), BF16),
            jax.ShapeDtypeStruct((BATCH * KV_BLOCKS, N_KV_HEADS * V_ROWS, TM_E), BF16),
            jax.ShapeDtypeStruct((N_TOK, 2 * D_FOURIER), F32),
        ],
        scratch_shapes=[pltpu.VMEM((D_KV, TM_E), F32),
                        pltpu.VMEM((D_MODEL // LANES, d_qkv, LANES), F32),
                        pltpu.VMEM((d_qkv, D_MODEL), BF16),
                        pltpu.VMEM((D_MODEL, D_FOURIER), BF16)],
        compiler_params=_params(48),
        name="proj_even",
    )(*toks, mod, g, w_in, gain, _ROPE_COS_T, _ROPE_SIN_T, *_CC, *_SC)


def _attn_kernel(par_ref, qt_ref, k_ref, vt_ref, o_ref, ot_scr):
    qi = pl.program_id(1)
    is_latent = qi < NQ
    use_bound = par_ref[0] > 0.5
    k_norm_bound = par_ref[1]
    blocks_per_chunk = CH_ONLINE // TM_E

    def group_consts(g):
        pair, side = divmod(g, 2)
        lanes = slice(LANES * pair, LANES * (pair + 1))
        rows = slice(V_ROWS * g, V_ROWS * (g + 1))
        qpads, qnorms = [], []
        for hh in range(GQA_GROUP):
            row0 = (GQA_GROUP * g + hh) * HEAD_DIM
            q_h = qt_ref[row0:row0 + HEAD_DIM, :]
            zero = jnp.zeros_like(q_h)
            qpads.append(jnp.concatenate([q_h, zero] if side == 0 else [zero, q_h], axis=0))
            qf = q_h.astype(F32)
            qnorms.append(jnp.sqrt(jnp.sum(qf * qf, axis=0, keepdims=True)))
        return lanes, rows, qpads, qnorms

    def pv(vt_blocks, p, acc):
        for j, vt in enumerate(vt_blocks):
            acc = acc + jnp.dot(vt, p[TM_E * j:TM_E * (j + 1), :], preferred_element_type=F32)
        return acc

    def store_heads(g, accs):
        for hh, acc in enumerate(accs):
            row0 = (GQA_GROUP * g + hh) * HEAD_DIM
            ot_scr[row0:row0 + HEAD_DIM, :] = acc[0:HEAD_DIM] / acc[HEAD_DIM:HEAD_DIM + 1]

    def online_step(k_chunk, vt_blocks, qpads, carries):
        sts = [jnp.dot(k_chunk, qpad, preferred_element_type=F32) for qpad in qpads]
        mid = []
        for st, (m, acc) in zip(sts, carries):
            m_new = jnp.maximum(m, jnp.max(st, axis=0, keepdims=True))
            mid.append((m_new, jnp.exp2(m - m_new), jnp.exp2(st - m_new).astype(BF16)))
        return tuple((m_new, pv(vt_blocks, p, alpha * acc))
                     for (m_new, alpha, p), (_, acc) in zip(mid, carries))

    def chunk_operands(c, lanes, rows):
        k0 = pl.multiple_of(c * CH_ONLINE, CH_ONLINE)
        return (k_ref[pl.ds(k0, CH_ONLINE), lanes],
                [vt_ref[c * blocks_per_chunk + j, rows, :] for j in range(blocks_per_chunk)])

    zero_acc = jnp.zeros((V_ROWS, TQ), F32)
    online_init = (jnp.full((1, TQ), -jnp.inf, F32), zero_acc)

    @pl.when(jnp.logical_and(is_latent, use_bound))
    def _():
        for g in range(N_KV_HEADS):
            lanes, rows, qpads, qnorms = group_consts(g)
            bounds = [qn * k_norm_bound for qn in qnorms]

            vt_blocks = [vt_ref[j, rows, :] for j in range(KV_BLOCKS)]
            sts = [jnp.dot(k_ref[:, lanes], qpad, preferred_element_type=F32) for qpad in qpads]
            ps = [jnp.exp2(st - u).astype(BF16) for st, u in zip(sts, bounds)]
            store_heads(g, [pv(vt_blocks, p, zero_acc) for p in ps])

    @pl.when(jnp.logical_and(is_latent, jnp.logical_not(use_bound)))
    def _():
        for g in range(N_KV_HEADS):
            lanes, rows, qpads, _ = group_consts(g)

            def body(c, carries, lanes=lanes, rows=rows, qpads=qpads):
                k_chunk, vt_blocks = chunk_operands(c, lanes, rows)
                return online_step(k_chunk, vt_blocks, qpads, carries)

            carries = lax.fori_loop(0, K_ROWS // CH_ONLINE, body, (online_init,) * GQA_GROUP)
            store_heads(g, [acc for _, acc in carries])

    @pl.when(jnp.logical_not(is_latent))
    def _():
        for g in range(N_KV_HEADS):
            lanes, rows, qpads, _ = group_consts(g)
            carries = online_step(k_ref[SEQ:K_ROWS, lanes], [vt_ref[KV_BLOCKS - 1, rows, :]], qpads,
                                  (online_init,) * GQA_GROUP)
            store_heads(g, [acc for _, acc in carries])

    o_ref[...] = ot_scr[...].T.astype(BF16)


def _attn_call(par, qt, k, vt3):
    def q_block(b, qi):
        return jnp.where(qi < NQ, b * NQ + qi, N_X // TQ + b)

    return pl.pallas_call(
        _attn_kernel,
        grid=(BATCH, NQ + 1),
        in_specs=[
            pl.BlockSpec(memory_space=pltpu.SMEM),
            pl.BlockSpec((D_Q, TQ), lambda b, qi: (0, q_block(b, qi))),
            pl.BlockSpec((K_ROWS, D_KV), lambda b, qi: (b, 0), pipeline_mode=pl.Buffered(1)),
            pl.BlockSpec((KV_BLOCKS, N_KV_HEADS * V_ROWS, TM_E), lambda b, qi: (b, 0, 0),
                         pipeline_mode=pl.Buffered(1)),
        ],
        out_specs=pl.BlockSpec((TQ, D_Q), lambda b, qi: (q_block(b, qi), 0)),
        out_shape=jax.ShapeDtypeStruct((N_TOK, D_Q), BF16),
        scratch_shapes=[pltpu.VMEM((D_Q, TQ), F32)],
        compiler_params=_params(58),
        name="attn",
    )(par, qt, k, vt3)


def _attn_params(q_gain, k_gain):
    slack = 1.0 + 2.0 ** -6
    k_norm_bound = (HEAD_DIM ** 0.5) * jnp.max(jnp.abs(k_gain)) * slack
    q_norm_bound = (HEAD_DIM ** 0.5) * jnp.max(jnp.abs(q_gain)) * Q_SCALE * slack
    use_bound = (q_norm_bound * k_norm_bound < MAX_SCORE_BOUND).astype(F32)
    return jnp.stack([use_bound, k_norm_bound]).astype(F32)


def _fourier1_kernel(w_ref, m_ref, z_ref):
    for j in range(TA):
        xa = w_ref[:, j, :]
        y = jnp.dot(m_ref[j], xa, precision=HIGHEST, preferred_element_type=F32)
        z_ref[j, 0:FFT_B, :] = y[0:FFT_B, 0:D_FOURIER] + y[FFT_B:, D_FOURIER:]
        z_ref[j, FFT_B:, :] = y[0:FFT_B, D_FOURIER:] - y[FFT_B:, 0:D_FOURIER]


def _fourier1_call(w):
    w3 = w.reshape(N_TOK // FFT_A, FFT_A, 2 * D_FOURIER)
    return pl.pallas_call(
        _fourier1_kernel,
        grid=(BATCH, FFT_A // TA),
        in_specs=[
            pl.BlockSpec((FFT_B, TA, 2 * D_FOURIER), lambda b, a: (b, a, 0)),
            pl.BlockSpec((TA, 2 * FFT_B, FFT_B), lambda b, a: (a, 0, 0)),
        ],
        out_specs=pl.BlockSpec((None, TA, 2 * FFT_B, D_FOURIER), lambda b, a: (b, a, 0, 0)),
        out_shape=jax.ShapeDtypeStruct((BATCH, FFT_A, 2 * FFT_B, D_FOURIER), F32),
        compiler_params=_params(32),
        name="fourier1",
    )(w3, _STAGE1)


def _fourier2_kernel(zr_ref, zi_ref, ch_ref, cl_ref, sh_ref, sl_ref, o_ref):
    zr = jnp.concatenate([zr_ref[:, j, :] for j in range(TB2)], axis=1)
    zi = jnp.concatenate([zi_ref[:, j, :] for j in range(TB2)], axis=1)
    res = (_dot3((ch_ref[...], cl_ref[...]), _split(zr))
           + _dot3((sh_ref[...], sl_ref[...]), _split(zi))) * FOURIER_SCALE_X
    for j in range(TB2):
        o_ref[:, j, :] = res[:, D_FOURIER * j:D_FOURIER * (j + 1)]


def _fourier2_call(z):
    n_j = FFT_B // TB2
    out = pl.pallas_call(
        _fourier2_kernel,
        grid=(BATCH, n_j),
        in_specs=[
            pl.BlockSpec((None, FFT_A, TB2, D_FOURIER), lambda b, j: (b, 0, j, 0)),
            pl.BlockSpec((None, FFT_A, TB2, D_FOURIER), lambda b, j: (b, 0, j + n_j, 0)),
        ] + [_const_spec((FFT_A, FFT_A))] * 4,
        out_specs=pl.BlockSpec((None, FFT_A, TB2, D_FOURIER), lambda b, j: (b, 0, j, 0)),
        out_shape=jax.ShapeDtypeStruct((BATCH, FFT_A, FFT_B, D_FOURIER), F32),
        compiler_params=_params(32),
        name="fourier2",
    )(z, z, *_C128, *_S128)
    return out.reshape(N_X, D_FOURIER)


def _fourier_ctx_kernel(w_ref, ch_ref, cl_ref, sh_ref, sl_ref, o_ref):
    w = w_ref[...]
    o_ref[...] = (_dot3((ch_ref[...], cl_ref[...]), _split(w[:, 0:D_FOURIER]))
                  + _dot3((sh_ref[...], sl_ref[...]), _split(w[:, D_FOURIER:]))) * FOURIER_SCALE_C


def _fourier_ctx_call(w):
    return pl.pallas_call(
        _fourier_ctx_kernel,
        grid=(BATCH,),
        in_specs=[
            pl.BlockSpec((CTX_LEN, 2 * D_FOURIER), lambda b: (N_X // CTX_LEN + b, 0)),
        ] + [_const_spec((CTX_LEN, CTX_LEN))] * 4,
        out_specs=pl.BlockSpec((CTX_LEN, D_FOURIER), lambda b: (b, 0)),
        out_shape=jax.ShapeDtypeStruct((N_C, D_FOURIER), F32),
        name="fourier_ctx",
    )(w, *_C256, *_S256)


def _proj_odd_kernel(x_ref, mod_ref, g_ref, w_ref, b_ref, u_ref):
    h = _norm_mod(x_ref[...], g_ref[...], mod_ref[0:1, :], mod_ref[1:2, :]).astype(BF16)
    a = jnp.dot(h, w_ref[...], preferred_element_type=F32) + b_ref[...]
    u_ref[...] = a[:, 0:D_MODEL] * jax.nn.sigmoid(a[:, D_MODEL:])


def _proj_odd_call(tok, mod, g, w, j, b, n_blocks):
    return pl.pallas_call(
        _proj_odd_kernel,
        grid=(n_blocks,),
        in_specs=[
            pl.BlockSpec((TM, D_MODEL), lambda i: (i, 0)),
            pl.BlockSpec((None, N_MOD, D_MODEL), lambda i: (_mod_block(i), 0, 0)),
            _const_spec((1, D_MODEL)),
            _layer_resident(w, j),
            _const_spec((1, 2 * D_MODEL)),
        ],
        out_specs=pl.BlockSpec((TM, D_MODEL), lambda i: (i, 0)),
        out_shape=jax.ShapeDtypeStruct((n_blocks * TM, D_MODEL), F32),
        compiler_params=_params(48),
        name="proj_odd",
    )(tok, mod, g, w, b)


def _dwconv_kernel(prev_ref, cur_ref, next_ref, w_ref, b_ref, lng_ref, lnb_ref, o_ref,
                   ext_scr, conv_scr):
    i = pl.program_id(0)
    blocks_per_seq = SEQ // CONV_TM
    is_ctx = i >= N_X // CONV_TM
    first = jnp.logical_or(is_ctx, i % blocks_per_seq == 0)
    last = jnp.logical_or(is_ctx, i % blocks_per_seq == blocks_per_seq - 1)
    ext_scr[0:CONV_HALO, :] = jnp.where(first, 0.0, prev_ref[...])
    ext_scr[CONV_HALO:CONV_HALO + CONV_TM, :] = cur_ref[...]
    ext_scr[CONV_HALO + CONV_TM:, :] = jnp.where(last, 0.0, next_ref[...])
    off = CONV_HALO - PAD
    win = CONV_RC + SUBLANES
    n_m = (CONV_WIDTH - 1 + off) // SUBLANES + 1
    for lc in range(D_MODEL // LANES):
        lanes = slice(LANES * lc, LANES * (lc + 1))
        for base in range(0, CONV_TM, CONV_RC):
            acc = jnp.broadcast_to(b_ref[:, lanes], (CONV_RC, LANES))
            for r in range(SUBLANES):
                q = None
                for m in range(n_m):
                    k = SUBLANES * m + r - off
                    if 0 <= k < CONV_WIDTH:
                        lo = base + SUBLANES * m
                        term = w_ref[k:k + 1, lanes] * ext_scr[lo:lo + win, lanes]
                        q = term if q is None else q + term
                acc = acc + q[r:r + CONV_RC]
            conv_scr[base:base + CONV_RC, lanes] = acc
    u = conv_scr[...]
    mu = jnp.mean(u, axis=-1, keepdims=True)
    d = u - mu
    var = jnp.mean(d * d, axis=-1, keepdims=True)
    y = d * lax.rsqrt(var + EPS) * lng_ref[...] + lnb_ref[...]
    o_ref[...] = (y * jax.nn.sigmoid(y)).astype(BF16)


def _dwconv_call(u, w_dw, b_dw, ln_g, ln_b):
    n_rows = u.shape[0]
    n_blocks = n_rows // CONV_TM
    halo_per_block = CONV_TM // CONV_HALO
    n_halo = n_rows // CONV_HALO
    return pl.pallas_call(
        _dwconv_kernel,
        grid=(n_blocks,),
        in_specs=[
            pl.BlockSpec((CONV_HALO, D_MODEL), lambda i: (jnp.maximum(i * halo_per_block - 1, 0), 0)),
            pl.BlockSpec((CONV_TM, D_MODEL), lambda i: (i, 0)),
            pl.BlockSpec((CONV_HALO, D_MODEL),
                         lambda i: (jnp.minimum((i + 1) * halo_per_block, n_halo - 1), 0)),
            _const_spec((CONV_WIDTH, D_MODEL)),
            _const_spec((1, D_MODEL)),
            _const_spec((1, D_MODEL)),
            _const_spec((1, D_MODEL)),
        ],
        out_specs=pl.BlockSpec((CONV_TM, D_MODEL), lambda i: (i, 0)),
        out_shape=jax.ShapeDtypeStruct((n_rows, D_MODEL), BF16),
        scratch_shapes=[pltpu.VMEM((CONV_TM + 2 * CONV_HALO, D_MODEL), F32),
                        pltpu.VMEM((CONV_TM, D_MODEL), F32)],
        compiler_params=_params(32),
        name="dwconv",
    )(u, u, u, w_dw, b_dw, ln_g, ln_b)


def _mlp_tail(x1, mod_ref, g2_ref, w1_ref, w2_ref, out_ref):
    h2 = _norm_mod(x1, g2_ref[...], mod_ref[3:4, :], mod_ref[4:5, :]).astype(BF16)
    acc = jnp.zeros((TM, D_MODEL), F32)
    for ck in range(D_FF // FF_CHUNK):
        cols = slice(FF_CHUNK * ck, FF_CHUNK * (ck + 1))
        a = jnp.maximum(jnp.dot(h2, w1_ref[:, cols], preferred_element_type=F32), 0.0)
        acc = acc + jnp.dot((a * a).astype(BF16), w2_ref[cols, :], preferred_element_type=F32)
    out_ref[...] = x1 + mod_ref[5:6, :] * acc


def _post_even_kernel(*refs, n_tok_refs):
    (mod_ref, o_ref, fx_ref, fc_ref, wo_ref, g2_ref, w1_ref, w2_ref, out_ref) = refs[n_tok_refs:]
    i = pl.program_id(0)
    fo = jnp.where(i >= X_BLOCKS, fc_ref[...], fx_ref[...]).astype(BF16)
    y = (jnp.dot(o_ref[...], wo_ref[0:D_Q, :], preferred_element_type=F32)
         + jnp.dot(fo, wo_ref[D_Q:, :], preferred_element_type=F32))
    x1 = _token_block(refs, n_tok_refs, X_BLOCKS) + mod_ref[2:3, :] * y
    _mlp_tail(x1, mod_ref, g2_ref, w1_ref, w2_ref, out_ref)


def _post_odd_kernel(x_ref, mod_ref, v_ref, wo_ref, bo_ref, g2_ref, w1_ref, w2_ref, out_ref):
    y = jnp.dot(v_ref[...], wo_ref[...], preferred_element_type=F32) + bo_ref[...]
    x1 = x_ref[...] + mod_ref[2:3, :] * y
    _mlp_tail(x1, mod_ref, g2_ref, w1_ref, w2_ref, out_ref)


def _resident(shape):
    nd = len(shape)
    return pl.BlockSpec(shape, lambda *_: (0,) * nd, pipeline_mode=pl.Buffered(1))


def _layer_resident(stack, layer):
    shape = stack.shape[1:]
    return pl.BlockSpec((None,) + shape, lambda *_: (layer,) + (0,) * len(shape),
                        pipeline_mode=pl.Buffered(1))


def _post_even_call(tok, mod, o, f_x, f_c, wo, j, g2, w1, w2, layer):
    toks, tok_specs = _token_specs(tok, TM)
    return pl.pallas_call(
        functools.partial(_post_even_kernel, n_tok_refs=len(toks)),
        grid=(TOK_BLOCKS,),
        in_specs=tok_specs + [
            pl.BlockSpec((None, N_MOD, D_MODEL), lambda i: (_mod_block(i), 0, 0)),
            pl.BlockSpec((TM, D_Q), lambda i: (i, 0)),
            pl.BlockSpec((TM, D_FOURIER), lambda i: (jnp.minimum(i, X_BLOCKS - 1), 0)),
            _const_spec((N_C, D_FOURIER)),
            _layer_resident(wo, j),
            _const_spec((1, D_MODEL)),
            _layer_resident(w1, layer),
            _layer_resident(w2, layer),
        ],
        out_specs=pl.BlockSpec((TM, D_MODEL), lambda i: (i, 0)),
        out_shape=jax.ShapeDtypeStruct((N_TOK, D_MODEL), F32),
        compiler_params=_params(56),
        name="post_even",
    )(*toks, mod, o, f_x, f_c, wo, g2, w1, w2)


def _post_odd_call(tok, mod, v, wo, j, bo, g2, w1, w2, layer, n_blocks):
    return pl.pallas_call(
        _post_odd_kernel,
        grid=(n_blocks,),
        in_specs=[
            pl.BlockSpec((TM, D_MODEL), lambda i: (i, 0)),
            pl.BlockSpec((None, N_MOD, D_MODEL), lambda i: (_mod_block(i), 0, 0)),
            pl.BlockSpec((TM, D_MODEL), lambda i: (i, 0)),
            _layer_resident(wo, j),
            _const_spec((1, D_MODEL)),
            _const_spec((1, D_MODEL)),
            _layer_resident(w1, layer),
            _layer_resident(w2, layer),
        ],
        out_specs=pl.BlockSpec((TM, D_MODEL), lambda i: (i, 0)),
        out_shape=jax.ShapeDtypeStruct((n_blocks * TM, D_MODEL), F32),
        compiler_params=_params(56),
        name="post_odd",
    )(tok, mod, v, wo, bo, g2, w1, w2)


def _split_halves(w_cols):
    lead = w_cols.shape[:-1]
    n_heads = w_cols.shape[-1] // HEAD_DIM
    w4 = w_cols.reshape(*lead, n_heads, HALF, 2)
    return jnp.swapaxes(w4, -1, -2).reshape(*lead, n_heads * HEAD_DIM)


def kernel(x, c, ctx, c_ctx, ada_w, ada_b, norm1_g, norm2_g, mlp_w1, mlp_w2, attn_w_in, q_norm_g, k_norm_g, attn_w_out, conv_w_pw1, conv_b_pw1, conv_w_dw, conv_b_dw, conv_ln_g, conv_ln_b, conv_w_pw2, conv_b_pw2):
    tok = (x.reshape(N_X, D_MODEL), ctx.reshape(N_C, D_MODEL))
    c8 = jnp.concatenate([c, c_ctx[None, :], jnp.zeros((SUBLANES - BATCH - 1, D_MODEL), F32)], axis=0)
    mods = _ada_call(c8, ada_w, ada_b)[:, :BATCH + 1].reshape(DEPTH, BATCH + 1, N_MOD, D_MODEL)

    w1 = mlp_w1.astype(BF16)
    w2 = mlp_w2.astype(BF16)
    w_out = attn_w_out.astype(BF16)
    w_pw1 = conv_w_pw1.astype(BF16)
    w_pw2 = conv_w_pw2.astype(BF16)

    for i in range(DEPTH):
        j = i // 2
        last = i == DEPTH - 1
        n_blocks = X_BLOCKS if last else TOK_BLOCKS
        mod = mods[i]
        g1 = norm1_g[i][None, :]
        g2 = norm2_g[i][None, :]
        if i % 2 == 0:
            gain = jnp.concatenate([jnp.tile(_split_halves(q_norm_g[j]), N_Q_HEADS),
                                    jnp.tile(_split_halves(k_norm_g[j]), N_KV_HEADS)])[:, None]
            qt, k, vt3, w = _proj_even_call(tok, mod, g1, attn_w_in, j, gain)
            o = _attn_call(_attn_params(q_norm_g[j], k_norm_g[j]), qt, k, vt3)
            f_x = _fourier2_call(_fourier1_call(w))
            f_c = _fourier_ctx_call(w)
            tok = _post_even_call(tok, mod, o, f_x, f_c, w_out, j, g2, w1, w2, i)
        else:
            u = _proj_odd_call(tok, mod, g1, w_pw1, j, conv_b_pw1[j][None, :], n_blocks)
            v = _dwconv_call(u, conv_w_dw[j], conv_b_dw[j][None, :], conv_ln_g[j][None, :],
                             conv_ln_b[j][None, :])
            tok = _post_odd_call(tok, mod, v, w_pw2, j, conv_b_pw2[j][None, :], g2, w1, w2, i,
                                 n_blocks)
    return tok[:N_X].reshape(BATCH, SEQ, D_MODEL)
```

```python
import functools

import ml_dtypes
import numpy as np
import jax
import jax.numpy as jnp
from jax import lax
from jax.experimental import pallas as pl
from jax.experimental.pallas import tpu as pltpu

D_MODEL = 1024
BATCH = 2
SEQ = 8192
DEPTH = 4
GRID_W = 64
CTX_LEN = 256
HEAD_DIM = 64
N_Q_HEADS = 12
N_KV_HEADS = 4
GQA_GROUP = N_Q_HEADS // N_KV_HEADS
D_Q = N_Q_HEADS * HEAD_DIM
D_KV = N_KV_HEADS * HEAD_DIM
D_FOURIER = 256
FOURIER_GROUP_DIM = 64
ROPE_PAIRS_PER_AXIS = HEAD_DIM // 4
ROPE_THETA = 10000.0
CONV_WIDTH = 31
D_FF = 4 * D_MODEL
N_MOD = 6
EPS = 1e-6
ATTN_SCALE = HEAD_DIM ** -0.5
LOG2_E = 1.4426950408889634
Q_SCALE = ATTN_SCALE * LOG2_E

N_X = BATCH * SEQ
N_C = BATCH * CTX_LEN
N_TOK = N_X + N_C
N_HEADS_QK = N_Q_HEADS + N_KV_HEADS
HALF = HEAD_DIM // 2

LANES = 128
SUBLANES = 8
MIB = 1024 * 1024

TM = 512
X_BLOCKS = N_X // TM
TOK_BLOCKS = N_TOK // TM
TM_E = CTX_LEN
TQ = CTX_LEN
NQ = SEQ // TQ
K_ROWS = SEQ + CTX_LEN
KV_BLOCKS = K_ROWS // TM_E
CH_ONLINE = 11 * TM_E
V_ROWS = HEAD_DIM + 16
MAX_SCORE_BOUND = 50.0
FF_CHUNK = 512
TN_ADA = 1536
FFT_A = 128
FFT_B = SEQ // FFT_A
TA = 8
TB2 = 8
CONV_TM = CTX_LEN
CONV_HALO = 16
CONV_RC = 128
PAD = CONV_WIDTH // 2

F32 = jnp.float32
BF16 = jnp.bfloat16
HIGHEST = lax.Precision.HIGHEST


def _rope_tables_t():
    freqs = (np.float32(ROPE_THETA) ** (-np.arange(ROPE_PAIRS_PER_AXIS, dtype=np.float32)
                                        / np.float32(ROPE_PAIRS_PER_AXIS))).astype(np.float32)
    t = np.arange(SEQ)
    row = (t // GRID_W).astype(np.float32)
    col = (t % GRID_W).astype(np.float32)
    ang = np.concatenate([row[:, None] * freqs, col[:, None] * freqs], axis=-1).astype(np.float32)
    cos = np.cos(ang.astype(np.float64)).T
    sin = np.sin(ang.astype(np.float64)).T
    cos = np.concatenate([cos, cos, np.ones((HALF, N_C))], axis=1)
    sin = np.concatenate([sin, sin, np.zeros((HALF, N_C))], axis=1)
    return cos.astype(np.float32), sin.astype(np.float32)


def _dft_cos_sin(n_out, n_in, period):
    k = np.arange(n_out)[:, None].astype(np.int64)
    n = np.arange(n_in)[None, :].astype(np.int64)
    ang = 2.0 * np.pi * ((k * n) % period).astype(np.float64) / period
    return np.cos(ang), np.sin(ang)


def _channel_dft_tables():
    c, s = _dft_cos_sin(FOURIER_GROUP_DIM, FOURIER_GROUP_DIM, FOURIER_GROUP_DIM)
    n_groups = D_FOURIER // FOURIER_GROUP_DIM
    cc = np.kron(np.eye(n_groups), c)
    sc = np.kron(np.eye(n_groups), s)
    return cc.astype(np.float32), sc.astype(np.float32)


def _stage1_tables():
    bp = np.arange(FFT_B)[None, :, None].astype(np.int64)
    a = np.arange(FFT_A)[:, None, None].astype(np.int64)
    b = np.arange(FFT_B)[None, None, :].astype(np.int64)
    ang = 2.0 * np.pi * ((bp * (a + FFT_A * b)) % SEQ).astype(np.float64) / SEQ
    return np.concatenate([np.cos(ang), np.sin(ang)], axis=1).astype(np.float32)


def _hi_lo(table):
    t32 = np.asarray(table, np.float32)
    hi = t32.astype(ml_dtypes.bfloat16)
    lo = (t32 - hi.astype(np.float32)).astype(ml_dtypes.bfloat16)
    return hi, lo


_ROPE_COS_T, _ROPE_SIN_T = _rope_tables_t()
_CC, _SC = (_hi_lo(t) for t in _channel_dft_tables())
_STAGE1 = _stage1_tables()
def _hi_lo_hi(table):
    hi, lo = _hi_lo(table)
    return np.concatenate([hi, lo, hi], axis=-1)


_CS128 = np.concatenate([_hi_lo_hi(t) for t in _dft_cos_sin(FFT_A, FFT_A, FFT_A)], axis=-1)
_C256, _S256 = (_hi_lo(t) for t in _dft_cos_sin(CTX_LEN, CTX_LEN, CTX_LEN))
FOURIER_SCALE_X = float((SEQ * FOURIER_GROUP_DIM) ** -0.5)
FOURIER_SCALE_C = float((CTX_LEN * FOURIER_GROUP_DIM) ** -0.5)


def _mod_block(i, tm=TM):
    return jnp.minimum(i // (SEQ // tm), BATCH)


def _const_spec(shape):
    nd = len(shape)
    return pl.BlockSpec(shape, lambda *_: (0,) * nd)


def _norm_mod(x, g, shift, scale):
    ms = jnp.mean(x * x, axis=-1, keepdims=True)
    return (x * lax.rsqrt(ms + EPS) * g) * (1.0 + scale) + shift


def _split(x):
    hi = x.astype(BF16)
    return hi, (x - hi.astype(F32)).astype(BF16)


def _dot3(a, b):
    def d(p, q):
        return jnp.dot(p, q, preferred_element_type=F32)
    return d(a[0], b[0]) + (d(a[1], b[0]) + d(a[0], b[1]))


def _params(vmem_mib):
    return pltpu.CompilerParams(vmem_limit_bytes=vmem_mib * MIB)


def _ada_kernel(c_ref, w_ref, b_ref, o_ref):
    c = c_ref[...]
    s = c * jax.nn.sigmoid(c)
    o_ref[...] = jnp.dot(s.astype(BF16), w_ref[...].astype(BF16),
                         preferred_element_type=F32) + b_ref[...]


def _ada_call(c8, ada_w, ada_b):
    n_out = N_MOD * D_MODEL
    return pl.pallas_call(
        _ada_kernel,
        grid=(DEPTH, n_out // TN_ADA),
        in_specs=[
            _const_spec((SUBLANES, D_MODEL)),
            pl.BlockSpec((None, D_MODEL, TN_ADA), lambda l, j: (l, 0, j)),
            pl.BlockSpec((None, 1, TN_ADA), lambda l, j: (l, 0, j)),
        ],
        out_specs=pl.BlockSpec((None, SUBLANES, TN_ADA), lambda l, j: (l, 0, j)),
        out_shape=jax.ShapeDtypeStruct((DEPTH, SUBLANES, n_out), F32),
        compiler_params=_params(40),
        name="ada",
    )(c8, ada_w, ada_b.reshape(DEPTH, 1, n_out))


def _kv_block(i):
    n_x = N_X // TM_E
    per_batch = SEQ // TM_E
    return jnp.where(i < n_x, (i // per_batch) * KV_BLOCKS + i % per_batch,
                     (i - n_x) * KV_BLOCKS + per_batch)


def _token_block(refs, n_tok_refs, n_latent_blocks):
    if n_tok_refs == 1:
        return refs[0][...]
    return jnp.where(pl.program_id(0) >= n_latent_blocks, refs[1][...], refs[0][...])


def _token_specs(tok, tm):
    if not isinstance(tok, tuple):
        return (tok,), [pl.BlockSpec((tm, D_MODEL), lambda i: (i, 0))]
    n_x = N_X // tm
    return tok, [pl.BlockSpec((tm, D_MODEL), lambda i: (jnp.minimum(i, n_x - 1), 0)),
                 pl.BlockSpec((tm, D_MODEL), lambda i: (jnp.maximum(i - n_x, 0), 0))]


def _proj_even_kernel(*refs, n_tok_refs):
    (mod_ref, g_ref, win_ref, gain_ref, cos_ref, sin_ref, cch_ref, ccl_ref, sch_ref, scl_ref,
     qt_ref, k_ref, vt_ref, w_ref, kt_scr, wt32_scr, wqkvt_scr, wf_scr) = refs[n_tok_refs:]
    d_qk = D_Q + D_KV
    d_qkv = D_Q + 2 * D_KV

    @pl.when(pl.program_id(0) == 0)
    def _():
        wt = win_ref[:, 0:d_qkv].T
        wqkvt_scr[d_qk:d_qkv, :] = wt[d_qk:d_qkv, :].astype(BF16)
        for c in range(D_MODEL // LANES):
            cols = slice(LANES * c, LANES * (c + 1))
            wt32_scr[c] = wt[:, cols]
            for hd in range(N_HEADS_QK):
                lo = HEAD_DIM * hd
                wqkvt_scr[lo:lo + HALF, cols] = wt32_scr[c, pl.ds(lo, HALF, stride=2), :].astype(BF16)
                wqkvt_scr[lo + HALF:lo + HEAD_DIM, cols] = (
                    wt32_scr[c, pl.ds(lo + 1, HALF, stride=2), :].astype(BF16))
        wf_scr[...] = win_ref[:, d_qkv:].astype(BF16)

    x = _token_block(refs, n_tok_refs, N_X // TM_E)
    h = _norm_mod(x, g_ref[...], mod_ref[0:1, :], mod_ref[1:2, :]).astype(BF16)
    t = lax.dot_general(wqkvt_scr[...], h, (((1,), (1,)), ((), ())), preferred_element_type=F32)
    cos = cos_ref[...]
    sin = sin_ref[...]
    for hd in range(N_HEADS_QK):
        lo = HEAD_DIM * hd
        x1 = t[lo:lo + HALF]
        x2 = t[lo + HALF:lo + HEAD_DIM]
        ms = (jnp.sum(x1 * x1, axis=0, keepdims=True)
              + jnp.sum(x2 * x2, axis=0, keepdims=True)) * (1.0 / HEAD_DIM)
        r = lax.rsqrt(ms + EPS)
        x1 = x1 * r * gain_ref[lo:lo + HALF, :]
        x2 = x2 * r * gain_ref[lo + HALF:lo + HEAD_DIM, :]
        o1 = x1 * cos - x2 * sin
        o2 = x1 * sin + x2 * cos
        if hd < N_Q_HEADS:
            qt_ref[lo:lo + HALF, :] = (o1 * Q_SCALE).astype(BF16)
            qt_ref[lo + HALF:lo + HEAD_DIM, :] = (o2 * Q_SCALE).astype(BF16)
        else:
            kt_scr[lo - D_Q:lo - D_Q + HALF, :] = o1
            kt_scr[lo - D_Q + HALF:lo - D_Q + HEAD_DIM, :] = o2
    k_ref[...] = kt_scr[...].T.astype(BF16)
    ones_rows = (lax.broadcasted_iota(jnp.int32, (V_ROWS - HEAD_DIM, TM_E), 0) == 0).astype(BF16)
    for g in range(N_KV_HEADS):
        src = D_Q + D_KV + HEAD_DIM * g
        vt_ref[V_ROWS * g:V_ROWS * g + HEAD_DIM, :] = t[src:src + HEAD_DIM].astype(BF16)
        vt_ref[V_ROWS * g + HEAD_DIM:V_ROWS * (g + 1), :] = ones_rows
    f = _split(jnp.dot(h, wf_scr[...], preferred_element_type=F32))
    w_ref[:, 0:D_FOURIER] = _dot3(f, (cch_ref[...], ccl_ref[...]))
    w_ref[:, D_FOURIER:2 * D_FOURIER] = -_dot3(f, (sch_ref[...], scl_ref[...]))


def _proj_even_call(tok, mod, g, w_in, j, gain):
    d_qkv = D_Q + 2 * D_KV
    toks, tok_specs = _token_specs(tok, TM_E)
    return pl.pallas_call(
        functools.partial(_proj_even_kernel, n_tok_refs=len(toks)),
        grid=(N_TOK // TM_E,),
        in_specs=tok_specs + [
            pl.BlockSpec((None, N_MOD, D_MODEL), lambda i: (_mod_block(i, TM_E), 0, 0)),
            _const_spec((1, D_MODEL)),
            _layer_resident(w_in, j),
            _const_spec((D_Q + D_KV, 1)),
            pl.BlockSpec((HALF, TM_E), lambda i: (0, i)),
            pl.BlockSpec((HALF, TM_E), lambda i: (0, i)),
        ] + [_const_spec((D_FOURIER, D_FOURIER))] * 4,
        out_specs=[
            pl.BlockSpec((D_Q, TM_E), lambda i: (0, i)),
            pl.BlockSpec((TM_E, D_KV), lambda i: (_kv_block(i), 0)),
            pl.BlockSpec((None, N_KV_HEADS * V_ROWS, TM_E), lambda i: (_kv_block(i), 0, 0)),
            pl.BlockSpec((TM_E, 2 * D_FOURIER), lambda i: (i, 0)),
        ],
        out_shape=[
            jax.ShapeDtypeStruct((D_Q, N_TOK), BF16),
            jax.ShapeDtypeStruct((BATCH * K_ROWS, D_KV), BF16),
            jax.ShapeDtypeStruct((BATCH * KV_BLOCKS, N_KV_HEADS * V_ROWS, TM_E), BF16),
            jax.ShapeDtypeStruct((N_TOK, 2 * D_FOURIER), F32),
        ],
        scratch_shapes=[pltpu.VMEM((D_KV, TM_E), F32),
                        pltpu.VMEM((D_MODEL // LANES, d_qkv, LANES), F32),
                        pltpu.VMEM((d_qkv, D_MODEL), BF16),
                        pltpu.VMEM((D_MODEL, D_FOURIER), BF16)],
        compiler_params=_params(48),
        name="proj_even",
    )(*toks, mod, g, w_in, gain, _ROPE_COS_T, _ROPE_SIN_T, *_CC, *_SC)


def _attn_kernel(par_ref, qt_ref, k_ref, vt_ref, o_ref, ot_scr):
    qi = pl.program_id(1)
    is_latent = qi < NQ
    use_bound = par_ref[0] > 0.5
    k_norm_bound = par_ref[1]
    blocks_per_chunk = CH_ONLINE // TM_E

    def group_consts(g):
        pair, side = divmod(g, 2)
        lanes = slice(LANES * pair, LANES * (pair + 1))
        rows = slice(V_ROWS * g, V_ROWS * (g + 1))
        qpads, qnorms = [], []
        for hh in range(GQA_GROUP):
            row0 = (GQA_GROUP * g + hh) * HEAD_DIM
            q_h = qt_ref[row0:row0 + HEAD_DIM, :]
            zero = jnp.zeros_like(q_h)
            qpads.append(jnp.concatenate([q_h, zero] if side == 0 else [zero, q_h], axis=0))
            qf = q_h.astype(F32)
            qnorms.append(jnp.sqrt(jnp.sum(qf * qf, axis=0, keepdims=True)))
        return lanes, rows, qpads, qnorms

    def pv(vt_blocks, p, acc):
        for j, vt in enumerate(vt_blocks):
            acc = acc + jnp.dot(vt, p[TM_E * j:TM_E * (j + 1), :], preferred_element_type=F32)
        return acc

    def store_heads(g, accs):
        for hh, acc in enumerate(accs):
            row0 = (GQA_GROUP * g + hh) * HEAD_DIM
            ot_scr[row0:row0 + HEAD_DIM, :] = acc[0:HEAD_DIM] / acc[HEAD_DIM:HEAD_DIM + 1]

    def online_step(k_chunk, vt_blocks, qpads, carries):
        sts = [jnp.dot(k_chunk, qpad, preferred_element_type=F32) for qpad in qpads]
        mid = []
        for st, (m, acc) in zip(sts, carries):
            m_new = jnp.maximum(m, jnp.max(st, axis=0, keepdims=True))
            mid.append((m_new, jnp.exp2(m - m_new), jnp.exp2(st - m_new).astype(BF16)))
        return tuple((m_new, pv(vt_blocks, p, alpha * acc))
                     for (m_new, alpha, p), (_, acc) in zip(mid, carries))

    def chunk_operands(c, lanes, rows):
        k0 = pl.multiple_of(c * CH_ONLINE, CH_ONLINE)
        return (k_ref[pl.ds(k0, CH_ONLINE), lanes],
                [vt_ref[c * blocks_per_chunk + j, rows, :] for j in range(blocks_per_chunk)])

    zero_acc = jnp.zeros((V_ROWS, TQ), F32)
    online_init = (jnp.full((1, TQ), -jnp.inf, F32), zero_acc)

    @pl.when(jnp.logical_and(is_latent, use_bound))
    def _():
        for g in range(N_KV_HEADS):
            lanes, rows, qpads, qnorms = group_consts(g)
            bounds = [qn * k_norm_bound for qn in qnorms]

            vt_blocks = [vt_ref[j, rows, :] for j in range(KV_BLOCKS)]
            sts = [jnp.dot(k_ref[:, lanes], qpad, preferred_element_type=F32) for qpad in qpads]
            ps = [jnp.exp2(st - u).astype(BF16) for st, u in zip(sts, bounds)]
            store_heads(g, [pv(vt_blocks, p, zero_acc) for p in ps])

    @pl.when(jnp.logical_and(is_latent, jnp.logical_not(use_bound)))
    def _():
        for g in range(N_KV_HEADS):
            lanes, rows, qpads, _ = group_consts(g)

            def body(c, carries, lanes=lanes, rows=rows, qpads=qpads):
                k_chunk, vt_blocks = chunk_operands(c, lanes, rows)
                return online_step(k_chunk, vt_blocks, qpads, carries)

            carries = lax.fori_loop(0, K_ROWS // CH_ONLINE, body, (online_init,) * GQA_GROUP)
            store_heads(g, [acc for _, acc in carries])

    @pl.when(jnp.logical_not(is_latent))
    def _():
        for g in range(N_KV_HEADS):
            lanes, rows, qpads, _ = group_consts(g)
            carries = online_step(k_ref[SEQ:K_ROWS, lanes], [vt_ref[KV_BLOCKS - 1, rows, :]], qpads,
                                  (online_init,) * GQA_GROUP)
            store_heads(g, [acc for _, acc in carries])

    o_ref[...] = ot_scr[...].T.astype(BF16)


def _attn_call(par, qt, k, vt3):
    def q_block(b, qi):
        return jnp.where(qi < NQ, b * NQ + qi, N_X // TQ + b)

    return pl.pallas_call(
        _attn_kernel,
        grid=(BATCH, NQ + 1),
        in_specs=[
            pl.BlockSpec(memory_space=pltpu.SMEM),
            pl.BlockSpec((D_Q, TQ), lambda b, qi: (0, q_block(b, qi))),
            pl.BlockSpec((K_ROWS, D_KV), lambda b, qi: (b, 0), pipeline_mode=pl.Buffered(1)),
            pl.BlockSpec((KV_BLOCKS, N_KV_HEADS * V_ROWS, TM_E), lambda b, qi: (b, 0, 0),
                         pipeline_mode=pl.Buffered(1)),
        ],
        out_specs=pl.BlockSpec((TQ, D_Q), lambda b, qi: (q_block(b, qi), 0)),
        out_shape=jax.ShapeDtypeStruct((N_TOK, D_Q), BF16),
        scratch_shapes=[pltpu.VMEM((D_Q, TQ), F32)],
        compiler_params=_params(58),
        name="attn",
    )(par, qt, k, vt3)


def _attn_params(q_gain, k_gain):
    slack = 1.0 + 2.0 ** -6
    k_norm_bound = (HEAD_DIM ** 0.5) * jnp.max(jnp.abs(k_gain)) * slack
    q_norm_bound = (HEAD_DIM ** 0.5) * jnp.max(jnp.abs(q_gain)) * Q_SCALE * slack
    use_bound = (q_norm_bound * k_norm_bound < MAX_SCORE_BOUND).astype(F32)
    return jnp.stack([use_bound, k_norm_bound]).astype(F32)


def _fourier1_kernel(w_ref, m_ref, z_ref):
    for j in range(TA):
        xa = w_ref[:, j, :]
        y = jnp.dot(m_ref[j], xa, precision=HIGHEST, preferred_element_type=F32)
        z_ref[j, 0:FFT_B, :] = y[0:FFT_B, 0:D_FOURIER] + y[FFT_B:, D_FOURIER:]
        z_ref[j, FFT_B:, :] = y[0:FFT_B, D_FOURIER:] - y[FFT_B:, 0:D_FOURIER]


def _fourier1_call(w):
    w3 = w.reshape(N_TOK // FFT_A, FFT_A, 2 * D_FOURIER)
    return pl.pallas_call(
        _fourier1_kernel,
        grid=(BATCH, FFT_A // TA),
        in_specs=[
            pl.BlockSpec((FFT_B, TA, 2 * D_FOURIER), lambda b, a: (b, a, 0)),
            pl.BlockSpec((TA, 2 * FFT_B, FFT_B), lambda b, a: (a, 0, 0)),
        ],
        out_specs=pl.BlockSpec((None, TA, 2 * FFT_B, D_FOURIER), lambda b, a: (b, a, 0, 0)),
        out_shape=jax.ShapeDtypeStruct((BATCH, FFT_A, 2 * FFT_B, D_FOURIER), F32),
        compiler_params=_params(32),
        name="fourier1",
    )(w3, _STAGE1)


def _fourier2_kernel(zr_ref, zi_ref, cs_ref, o_ref):
    zr = jnp.concatenate([zr_ref[:, j, :] for j in range(TB2)], axis=1)
    zi = jnp.concatenate([zi_ref[:, j, :] for j in range(TB2)], axis=1)
    (rh, rl), (ih, il) = _split(zr), _split(zi)
    rhs = jnp.concatenate([rh, rh, rl, ih, ih, il], axis=0)
    res = jnp.dot(cs_ref[...], rhs, preferred_element_type=F32) * FOURIER_SCALE_X
    for j in range(TB2):
        o_ref[:, j, :] = res[:, D_FOURIER * j:D_FOURIER * (j + 1)]


def _fourier2_call(z):
    n_j = FFT_B // TB2
    out = pl.pallas_call(
        _fourier2_kernel,
        grid=(BATCH, n_j),
        in_specs=[
            pl.BlockSpec((None, FFT_A, TB2, D_FOURIER), lambda b, j: (b, 0, j, 0)),
            pl.BlockSpec((None, FFT_A, TB2, D_FOURIER), lambda b, j: (b, 0, j + n_j, 0)),
            _const_spec((FFT_A, 6 * FFT_A)),
        ],
        out_specs=pl.BlockSpec((None, FFT_A, TB2, D_FOURIER), lambda b, j: (b, 0, j, 0)),
        out_shape=jax.ShapeDtypeStruct((BATCH, FFT_A, FFT_B, D_FOURIER), F32),
        compiler_params=_params(32),
        name="fourier2",
    )(z, z, _CS128)
    return out.reshape(N_X, D_FOURIER)


def _fourier_ctx_kernel(w_ref, ch_ref, cl_ref, sh_ref, sl_ref, o_ref):
    w = w_ref[...]
    o_ref[...] = (_dot3((ch_ref[...], cl_ref[...]), _split(w[:, 0:D_FOURIER]))
                  + _dot3((sh_ref[...], sl_ref[...]), _split(w[:, D_FOURIER:]))) * FOURIER_SCALE_C


def _fourier_ctx_call(w):
    return pl.pallas_call(
        _fourier_ctx_kernel,
        grid=(BATCH,),
        in_specs=[
            pl.BlockSpec((CTX_LEN, 2 * D_FOURIER), lambda b: (N_X // CTX_LEN + b, 0)),
        ] + [_const_spec((CTX_LEN, CTX_LEN))] * 4,
        out_specs=pl.BlockSpec((CTX_LEN, D_FOURIER), lambda b: (b, 0)),
        out_shape=jax.ShapeDtypeStruct((N_C, D_FOURIER), F32),
        name="fourier_ctx",
    )(w, *_C256, *_S256)


def _proj_odd_kernel(x_ref, mod_ref, g_ref, w_ref, b_ref, u_ref):
    h = _norm_mod(x_ref[...], g_ref[...], mod_ref[0:1, :], mod_ref[1:2, :]).astype(BF16)
    a = jnp.dot(h, w_ref[...], preferred_element_type=F32) + b_ref[...]
    u_ref[...] = a[:, 0:D_MODEL] * jax.nn.sigmoid(a[:, D_MODEL:])


def _proj_odd_call(tok, mod, g, w, j, b, n_blocks):
    return pl.pallas_call(
        _proj_odd_kernel,
        grid=(n_blocks,),
        in_specs=[
            pl.BlockSpec((TM, D_MODEL), lambda i: (i, 0)),
            pl.BlockSpec((None, N_MOD, D_MODEL), lambda i: (_mod_block(i), 0, 0)),
            _const_spec((1, D_MODEL)),
            _layer_resident(w, j),
            _const_spec((1, 2 * D_MODEL)),
        ],
        out_specs=pl.BlockSpec((TM, D_MODEL), lambda i: (i, 0)),
        out_shape=jax.ShapeDtypeStruct((n_blocks * TM, D_MODEL), F32),
        compiler_params=_params(48),
        name="proj_odd",
    )(tok, mod, g, w, b)


def _dwconv_kernel(prev_ref, cur_ref, next_ref, w_ref, b_ref, lng_ref, lnb_ref, o_ref,
                   ext_scr, conv_scr):
    i = pl.program_id(0)
    blocks_per_seq = SEQ // CONV_TM
    is_ctx = i >= N_X // CONV_TM
    first = jnp.logical_or(is_ctx, i % blocks_per_seq == 0)
    last = jnp.logical_or(is_ctx, i % blocks_per_seq == blocks_per_seq - 1)
    ext_scr[0:CONV_HALO, :] = jnp.where(first, 0.0, prev_ref[...])
    ext_scr[CONV_HALO:CONV_HALO + CONV_TM, :] = cur_ref[...]
    ext_scr[CONV_HALO + CONV_TM:, :] = jnp.where(last, 0.0, next_ref[...])
    off = CONV_HALO - PAD
    win = CONV_RC + SUBLANES
    n_m = (CONV_WIDTH - 1 + off) // SUBLANES + 1
    for lc in range(D_MODEL // LANES):
        lanes = slice(LANES * lc, LANES * (lc + 1))
        for base in range(0, CONV_TM, CONV_RC):
            acc = jnp.broadcast_to(b_ref[:, lanes], (CONV_RC, LANES))
            for r in range(SUBLANES):
                q = None
                for m in range(n_m):
                    k = SUBLANES * m + r - off
                    if 0 <= k < CONV_WIDTH:
                        lo = base + SUBLANES * m
                        term = w_ref[k:k + 1, lanes] * ext_scr[lo:lo + win, lanes]
                        q = term if q is None else q + term
                acc = acc + q[r:r + CONV_RC]
            conv_scr[base:base + CONV_RC, lanes] = acc
    u = conv_scr[...]
    mu = jnp.mean(u, axis=-1, keepdims=True)
    d = u - mu
    var = jnp.mean(d * d, axis=-1, keepdims=True)
    y = d * lax.rsqrt(var + EPS) * lng_ref[...] + lnb_ref[...]
    o_ref[...] = (y * jax.nn.sigmoid(y)).astype(BF16)


def _dwconv_call(u, w_dw, b_dw, ln_g, ln_b):
    n_rows = u.shape[0]
    n_blocks = n_rows // CONV_TM
    halo_per_block = CONV_TM // CONV_HALO
    n_halo = n_rows // CONV_HALO
    return pl.pallas_call(
        _dwconv_kernel,
        grid=(n_blocks,),
        in_specs=[
            pl.BlockSpec((CONV_HALO, D_MODEL), lambda i: (jnp.maximum(i * halo_per_block - 1, 0), 0)),
            pl.BlockSpec((CONV_TM, D_MODEL), lambda i: (i, 0)),
            pl.BlockSpec((CONV_HALO, D_MODEL),
                         lambda i: (jnp.minimum((i + 1) * halo_per_block, n_halo - 1), 0)),
            _const_spec((CONV_WIDTH, D_MODEL)),
            _const_spec((1, D_MODEL)),
            _const_spec((1, D_MODEL)),
            _const_spec((1, D_MODEL)),
        ],
        out_specs=pl.BlockSpec((CONV_TM, D_MODEL), lambda i: (i, 0)),
        out_shape=jax.ShapeDtypeStruct((n_rows, D_MODEL), BF16),
        scratch_shapes=[pltpu.VMEM((CONV_TM + 2 * CONV_HALO, D_MODEL), F32),
                        pltpu.VMEM((CONV_TM, D_MODEL), F32)],
        compiler_params=_params(32),
        name="dwconv",
    )(u, u, u, w_dw, b_dw, ln_g, ln_b)


def _mlp_tail(x1, mod_ref, g2_ref, w1_ref, w2_ref, out_ref):
    h2 = _norm_mod(x1, g2_ref[...], mod_ref[3:4, :], mod_ref[4:5, :]).astype(BF16)
    acc = jnp.zeros((TM, D_MODEL), F32)
    for ck in range(D_FF // FF_CHUNK):
        cols = slice(FF_CHUNK * ck, FF_CHUNK * (ck + 1))
        a = jnp.maximum(jnp.dot(h2, w1_ref[:, cols], preferred_element_type=F32), 0.0)
        acc = acc + jnp.dot((a * a).astype(BF16), w2_ref[cols, :], preferred_element_type=F32)
    out_ref[...] = x1 + mod_ref[5:6, :] * acc


def _post_even_kernel(*refs, n_tok_refs):
    (mod_ref, o_ref, fx_ref, fc_ref, wo_ref, g2_ref, w1_ref, w2_ref, out_ref) = refs[n_tok_refs:]
    i = pl.program_id(0)
    fo = jnp.where(i >= X_BLOCKS, fc_ref[...], fx_ref[...]).astype(BF16)
    y = (jnp.dot(o_ref[...], wo_ref[0:D_Q, :], preferred_element_type=F32)
         + jnp.dot(fo, wo_ref[D_Q:, :], preferred_element_type=F32))
    x1 = _token_block(refs, n_tok_refs, X_BLOCKS) + mod_ref[2:3, :] * y
    _mlp_tail(x1, mod_ref, g2_ref, w1_ref, w2_ref, out_ref)


def _post_odd_kernel(x_ref, mod_ref, v_ref, wo_ref, bo_ref, g2_ref, w1_ref, w2_ref, out_ref):
    y = jnp.dot(v_ref[...], wo_ref[...], preferred_element_type=F32) + bo_ref[...]
    x1 = x_ref[...] + mod_ref[2:3, :] * y
    _mlp_tail(x1, mod_ref, g2_ref, w1_ref, w2_ref, out_ref)


def _layer_resident(stack, layer):
    shape = stack.shape[1:]
    return pl.BlockSpec((None,) + shape, lambda *_: (layer,) + (0,) * len(shape),
                        pipeline_mode=pl.Buffered(1))


def _post_even_call(tok, mod, o, f_x, f_c, wo, j, g2, w1, w2, layer):
    toks, tok_specs = _token_specs(tok, TM)
    return pl.pallas_call(
        functools.partial(_post_even_kernel, n_tok_refs=len(toks)),
        grid=(TOK_BLOCKS,),
        in_specs=tok_specs + [
            pl.BlockSpec((None, N_MOD, D_MODEL), lambda i: (_mod_block(i), 0, 0)),
            pl.BlockSpec((TM, D_Q), lambda i: (i, 0)),
            pl.BlockSpec((TM, D_FOURIER), lambda i: (jnp.minimum(i, X_BLOCKS - 1), 0)),
            _const_spec((N_C, D_FOURIER)),
            _layer_resident(wo, j),
            _const_spec((1, D_MODEL)),
            _layer_resident(w1, layer),
            _layer_resident(w2, layer),
        ],
        out_specs=pl.BlockSpec((TM, D_MODEL), lambda i: (i, 0)),
        out_shape=jax.ShapeDtypeStruct((N_TOK, D_MODEL), F32),
        compiler_params=_params(56),
        name="post_even",
    )(*toks, mod, o, f_x, f_c, wo, g2, w1, w2)


def _post_odd_call(tok, mod, v, wo, j, bo, g2, w1, w2, layer, n_blocks):
    return pl.pallas_call(
        _post_odd_kernel,
        grid=(n_blocks,),
        in_specs=[
            pl.BlockSpec((TM, D_MODEL), lambda i: (i, 0)),
            pl.BlockSpec((None, N_MOD, D_MODEL), lambda i: (_mod_block(i), 0, 0)),
            pl.BlockSpec((TM, D_MODEL), lambda i: (i, 0)),
            _layer_resident(wo, j),
            _const_spec((1, D_MODEL)),
            _const_spec((1, D_MODEL)),
            _layer_resident(w1, layer),
            _layer_resident(w2, layer),
        ],
        out_specs=pl.BlockSpec((TM, D_MODEL), lambda i: (i, 0)),
        out_shape=jax.ShapeDtypeStruct((n_blocks * TM, D_MODEL), F32),
        compiler_params=_params(56),
        name="post_odd",
    )(tok, mod, v, wo, bo, g2, w1, w2)


def _split_halves(w_cols):
    lead = w_cols.shape[:-1]
    n_heads = w_cols.shape[-1] // HEAD_DIM
    w4 = w_cols.reshape(*lead, n_heads, HALF, 2)
    return jnp.swapaxes(w4, -1, -2).reshape(*lead, n_heads * HEAD_DIM)


def kernel(x, c, ctx, c_ctx, ada_w, ada_b, norm1_g, norm2_g, mlp_w1, mlp_w2, attn_w_in, q_norm_g, k_norm_g, attn_w_out, conv_w_pw1, conv_b_pw1, conv_w_dw, conv_b_dw, conv_ln_g, conv_ln_b, conv_w_pw2, conv_b_pw2):
    tok = (x.reshape(N_X, D_MODEL), ctx.reshape(N_C, D_MODEL))
    c8 = jnp.concatenate([c, c_ctx[None, :], jnp.zeros((SUBLANES - BATCH - 1, D_MODEL), F32)], axis=0)
    mods = _ada_call(c8, ada_w, ada_b)[:, :BATCH + 1].reshape(DEPTH, BATCH + 1, N_MOD, D_MODEL)

    w1 = mlp_w1.astype(BF16)
    w2 = mlp_w2.astype(BF16)
    w_out = attn_w_out.astype(BF16)
    w_pw1 = conv_w_pw1.astype(BF16)
    w_pw2 = conv_w_pw2.astype(BF16)

    for i in range(DEPTH):
        j = i // 2
        last = i == DEPTH - 1
        n_blocks = X_BLOCKS if last else TOK_BLOCKS
        mod = mods[i]
        g1 = norm1_g[i][None, :]
        g2 = norm2_g[i][None, :]
        if i % 2 == 0:
            gain = jnp.concatenate([jnp.tile(_split_halves(q_norm_g[j]), N_Q_HEADS),
                                    jnp.tile(_split_halves(k_norm_g[j]), N_KV_HEADS)])[:, None]
            qt, k, vt3, w = _proj_even_call(tok, mod, g1, attn_w_in, j, gain)
            o = _attn_call(_attn_params(q_norm_g[j], k_norm_g[j]), qt, k, vt3)
            f_x = _fourier2_call(_fourier1_call(w))
            f_c = _fourier_ctx_call(w)
            tok = _post_even_call(tok, mod, o, f_x, f_c, w_out, j, g2, w1, w2, i)
        else:
            u = _proj_odd_call(tok, mod, g1, w_pw1, j, conv_b_pw1[j][None, :], n_blocks)
            v = _dwconv_call(u, conv_w_dw[j], conv_b_dw[j][None, :], conv_ln_g[j][None, :],
                             conv_ln_b[j][None, :])
            tok = _post_odd_call(tok, mod, v, w_pw2, j, conv_b_pw2[j][None, :], g2, w1, w2, i,
                                 n_blocks)
    return tok[:N_X].reshape(BATCH, SEQ, D_MODEL)
```

```python
import functools

import ml_dtypes
import numpy as np
import jax
import jax.numpy as jnp
from jax import lax
from jax.experimental import pallas as pl
from jax.experimental.pallas import tpu as pltpu

D_MODEL = 1024
BATCH = 2
SEQ = 8192
DEPTH = 4
GRID_W = 64
CTX_LEN = 256
HEAD_DIM = 64
N_Q_HEADS = 12
N_KV_HEADS = 4
GQA_GROUP = N_Q_HEADS // N_KV_HEADS
D_Q = N_Q_HEADS * HEAD_DIM
D_KV = N_KV_HEADS * HEAD_DIM
D_FOURIER = 256
FOURIER_GROUP_DIM = 64
ROPE_PAIRS_PER_AXIS = HEAD_DIM // 4
ROPE_THETA = 10000.0
CONV_WIDTH = 31
D_FF = 4 * D_MODEL
N_MOD = 6
EPS = 1e-6
ATTN_SCALE = HEAD_DIM ** -0.5
LOG2_E = 1.4426950408889634
Q_SCALE = ATTN_SCALE * LOG2_E

N_X = BATCH * SEQ
N_C = BATCH * CTX_LEN
N_TOK = N_X + N_C
N_HEADS_QK = N_Q_HEADS + N_KV_HEADS
HALF = HEAD_DIM // 2

LANES = 128
SUBLANES = 8
MIB = 1024 * 1024

TM = 512
X_BLOCKS = N_X // TM
TOK_BLOCKS = N_TOK // TM
TM_E = CTX_LEN
TQ = CTX_LEN
NQ = SEQ // TQ
K_ROWS = SEQ + CTX_LEN
KV_BLOCKS = K_ROWS // TM_E
CH_ONLINE = 11 * TM_E
V_ROWS = HEAD_DIM + 16
MAX_SCORE_BOUND = 50.0
FF_CHUNK = 512
TN_ADA = 1536
FFT_A = 128
FFT_B = SEQ // FFT_A
TA = 8
TB2 = 8
CONV_TM = CTX_LEN
CONV_HALO = 16
CONV_RC = 128
PAD = CONV_WIDTH // 2

F32 = jnp.float32
BF16 = jnp.bfloat16
HIGHEST = lax.Precision.HIGHEST


def _rope_tables_t():
    freqs = (np.float32(ROPE_THETA) ** (-np.arange(ROPE_PAIRS_PER_AXIS, dtype=np.float32)
                                        / np.float32(ROPE_PAIRS_PER_AXIS))).astype(np.float32)
    t = np.arange(SEQ)
    row = (t // GRID_W).astype(np.float32)
    col = (t % GRID_W).astype(np.float32)
    ang = np.concatenate([row[:, None] * freqs, col[:, None] * freqs], axis=-1).astype(np.float32)
    cos = np.cos(ang.astype(np.float64)).T
    sin = np.sin(ang.astype(np.float64)).T
    cos = np.concatenate([cos, cos, np.ones((HALF, N_C))], axis=1)
    sin = np.concatenate([sin, sin, np.zeros((HALF, N_C))], axis=1)
    return cos.astype(np.float32), sin.astype(np.float32)


def _dft_cos_sin(n_out, n_in, period):
    k = np.arange(n_out)[:, None].astype(np.int64)
    n = np.arange(n_in)[None, :].astype(np.int64)
    ang = 2.0 * np.pi * ((k * n) % period).astype(np.float64) / period
    return np.cos(ang), np.sin(ang)


def _channel_dft_tables():
    c, s = _dft_cos_sin(FOURIER_GROUP_DIM, FOURIER_GROUP_DIM, FOURIER_GROUP_DIM)
    n_groups = D_FOURIER // FOURIER_GROUP_DIM
    cc = np.kron(np.eye(n_groups), c)
    sc = np.kron(np.eye(n_groups), s)
    return cc.astype(np.float32), sc.astype(np.float32)


def _stage1_tables():
    bp = np.arange(FFT_B)[None, :, None].astype(np.int64)
    a = np.arange(FFT_A)[:, None, None].astype(np.int64)
    b = np.arange(FFT_B)[None, None, :].astype(np.int64)
    ang = 2.0 * np.pi * ((bp * (a + FFT_A * b)) % SEQ).astype(np.float64) / SEQ
    return np.concatenate([np.cos(ang), np.sin(ang)], axis=1).astype(np.float32)


def _hi_lo(table):
    t32 = np.asarray(table, np.float32)
    hi = t32.astype(ml_dtypes.bfloat16)
    lo = (t32 - hi.astype(np.float32)).astype(ml_dtypes.bfloat16)
    return hi, lo


_ROPE_COS_T, _ROPE_SIN_T = _rope_tables_t()
_CC, _SC = (_hi_lo(t) for t in _channel_dft_tables())
_STAGE1 = _stage1_tables()
def _hi_lo_hi(table):
    hi, lo = _hi_lo(table)
    return np.concatenate([hi, lo, hi], axis=-1)


_CS128 = np.concatenate([_hi_lo_hi(t) for t in _dft_cos_sin(FFT_A, FFT_A, FFT_A)], axis=-1)
_C256, _S256 = (_hi_lo(t) for t in _dft_cos_sin(CTX_LEN, CTX_LEN, CTX_LEN))
FOURIER_SCALE_X = float((SEQ * FOURIER_GROUP_DIM) ** -0.5)
FOURIER_SCALE_C = float((CTX_LEN * FOURIER_GROUP_DIM) ** -0.5)


def _mod_block(i, tm=TM):
    return jnp.minimum(i // (SEQ // tm), BATCH)


def _const_spec(shape):
    nd = len(shape)
    return pl.BlockSpec(shape, lambda *_: (0,) * nd)


def _norm_mod(x, g, shift, scale):
    ms = jnp.mean(x * x, axis=-1, keepdims=True)
    return (x * lax.rsqrt(ms + EPS) * g) * (1.0 + scale) + shift


def _split(x):
    hi = x.astype(BF16)
    return hi, (x - hi.astype(F32)).astype(BF16)


def _dot3(a, b):
    def d(p, q):
        return jnp.dot(p, q, preferred_element_type=F32)
    return d(a[0], b[0]) + (d(a[1], b[0]) + d(a[0], b[1]))


def _params(vmem_mib):
    return pltpu.CompilerParams(vmem_limit_bytes=vmem_mib * MIB)


def _ada_kernel(c_ref, w_ref, b_ref, o_ref):
    c = c_ref[...]
    s = c * jax.nn.sigmoid(c)
    o_ref[...] = jnp.dot(s.astype(BF16), w_ref[...].astype(BF16),
                         preferred_element_type=F32) + b_ref[...]


def _ada_call(c8, ada_w, ada_b):
    n_out = N_MOD * D_MODEL
    return pl.pallas_call(
        _ada_kernel,
        grid=(DEPTH, n_out // TN_ADA),
        in_specs=[
            _const_spec((SUBLANES, D_MODEL)),
            pl.BlockSpec((None, D_MODEL, TN_ADA), lambda l, j: (l, 0, j)),
            pl.BlockSpec((None, 1, TN_ADA), lambda l, j: (l, 0, j)),
        ],
        out_specs=pl.BlockSpec((None, SUBLANES, TN_ADA), lambda l, j: (l, 0, j)),
        out_shape=jax.ShapeDtypeStruct((DEPTH, SUBLANES, n_out), F32),
        compiler_params=_params(40),
        name="ada",
    )(c8, ada_w, ada_b.reshape(DEPTH, 1, n_out))


def _kv_block(i):
    n_x = N_X // TM_E
    per_batch = SEQ // TM_E
    return jnp.where(i < n_x, (i // per_batch) * KV_BLOCKS + i % per_batch,
                     (i - n_x) * KV_BLOCKS + per_batch)


def _token_block(refs, n_tok_refs, n_latent_blocks):
    if n_tok_refs == 1:
        return refs[0][...]
    return jnp.where(pl.program_id(0) >= n_latent_blocks, refs[1][...], refs[0][...])


def _token_specs(tok, tm):
    if not isinstance(tok, tuple):
        return (tok,), [pl.BlockSpec((tm, D_MODEL), lambda i: (i, 0))]
    n_x = N_X // tm
    return tok, [pl.BlockSpec((tm, D_MODEL), lambda i: (jnp.minimum(i, n_x - 1), 0)),
                 pl.BlockSpec((tm, D_MODEL), lambda i: (jnp.maximum(i - n_x, 0), 0))]


def _proj_even_kernel(*refs, n_tok_refs):
    (mod_ref, g_ref, win_ref, gain_ref, cos_ref, sin_ref, cch_ref, ccl_ref, sch_ref, scl_ref,
     qt_ref, k_ref, vt_ref, w_ref, kt_scr, wt32_scr, wqkvt_scr, wf_scr) = refs[n_tok_refs:]
    d_qk = D_Q + D_KV
    d_qkv = D_Q + 2 * D_KV

    @pl.when(pl.program_id(0) == 0)
    def _():
        wt = win_ref[:, 0:d_qkv].T
        wqkvt_scr[d_qk:d_qkv, :] = wt[d_qk:d_qkv, :].astype(BF16)
        for c in range(D_MODEL // LANES):
            cols = slice(LANES * c, LANES * (c + 1))
            wt32_scr[c] = wt[:, cols]
            for hd in range(N_HEADS_QK):
                lo = HEAD_DIM * hd
                wqkvt_scr[lo:lo + HALF, cols] = wt32_scr[c, pl.ds(lo, HALF, stride=2), :].astype(BF16)
                wqkvt_scr[lo + HALF:lo + HEAD_DIM, cols] = (
                    wt32_scr[c, pl.ds(lo + 1, HALF, stride=2), :].astype(BF16))
        wf_scr[...] = win_ref[:, d_qkv:].astype(BF16)

    x = _token_block(refs, n_tok_refs, N_X // TM_E)
    h = _norm_mod(x, g_ref[...], mod_ref[0:1, :], mod_ref[1:2, :]).astype(BF16)
    t = lax.dot_general(wqkvt_scr[...], h, (((1,), (1,)), ((), ())), preferred_element_type=F32)
    cos = cos_ref[...]
    sin = sin_ref[...]
    for hd in range(N_HEADS_QK):
        lo = HEAD_DIM * hd
        x1 = t[lo:lo + HALF]
        x2 = t[lo + HALF:lo + HEAD_DIM]
        ms = (jnp.sum(x1 * x1, axis=0, keepdims=True)
              + jnp.sum(x2 * x2, axis=0, keepdims=True)) * (1.0 / HEAD_DIM)
        r = lax.rsqrt(ms + EPS)
        x1 = x1 * r * gain_ref[lo:lo + HALF, :]
        x2 = x2 * r * gain_ref[lo + HALF:lo + HEAD_DIM, :]
        o1 = x1 * cos - x2 * sin
        o2 = x1 * sin + x2 * cos
        if hd < N_Q_HEADS:
            qt_ref[lo:lo + HALF, :] = (o1 * Q_SCALE).astype(BF16)
            qt_ref[lo + HALF:lo + HEAD_DIM, :] = (o2 * Q_SCALE).astype(BF16)
        else:
            kt_scr[lo - D_Q:lo - D_Q + HALF, :] = o1
            kt_scr[lo - D_Q + HALF:lo - D_Q + HEAD_DIM, :] = o2
    k_ref[...] = kt_scr[...].T.astype(BF16)
    ones_rows = (lax.broadcasted_iota(jnp.int32, (V_ROWS - HEAD_DIM, TM_E), 0) == 0).astype(BF16)
    for g in range(N_KV_HEADS):
        src = D_Q + D_KV + HEAD_DIM * g
        vt_ref[V_ROWS * g:V_ROWS * g + HEAD_DIM, :] = t[src:src + HEAD_DIM].astype(BF16)
        vt_ref[V_ROWS * g + HEAD_DIM:V_ROWS * (g + 1), :] = ones_rows
    f = _split(jnp.dot(h, wf_scr[...], preferred_element_type=F32))
    w_ref[:, 0:D_FOURIER] = _dot3(f, (cch_ref[...], ccl_ref[...]))
    w_ref[:, D_FOURIER:2 * D_FOURIER] = -_dot3(f, (sch_ref[...], scl_ref[...]))


def _proj_even_call(tok, mod, g, w_in, j, gain):
    d_qkv = D_Q + 2 * D_KV
    toks, tok_specs = _token_specs(tok, TM_E)
    return pl.pallas_call(
        functools.partial(_proj_even_kernel, n_tok_refs=len(toks)),
        grid=(N_TOK // TM_E,),
        in_specs=tok_specs + [
            pl.BlockSpec((None, N_MOD, D_MODEL), lambda i: (_mod_block(i, TM_E), 0, 0)),
            _const_spec((1, D_MODEL)),
            _layer_resident(w_in, j),
            _const_spec((D_Q + D_KV, 1)),
            pl.BlockSpec((HALF, TM_E), lambda i: (0, i)),
            pl.BlockSpec((HALF, TM_E), lambda i: (0, i)),
        ] + [_const_spec((D_FOURIER, D_FOURIER))] * 4,
        out_specs=[
            pl.BlockSpec((D_Q, TM_E), lambda i: (0, i)),
            pl.BlockSpec((TM_E, D_KV), lambda i: (_kv_block(i), 0)),
            pl.BlockSpec((None, N_KV_HEADS * V_ROWS, TM_E), lambda i: (_kv_block(i), 0, 0)),
            pl.BlockSpec((TM_E, 2 * D_FOURIER), lambda i: (i, 0)),
        ],
        out_shape=[
            jax.ShapeDtypeStruct((D_Q, N_TOK), BF16),
            jax.ShapeDtypeStruct((BATCH * K_ROWS, D_KV), BF16),
            jax.ShapeDtypeStruct((BATCH * KV_BLOCKS, N_KV_HEADS * V_ROWS, TM_E), BF16),
            jax.ShapeDtypeStruct((N_TOK, 2 * D_FOURIER), F32),
        ],
        scratch_shapes=[pltpu.VMEM((D_KV, TM_E), F32),
                        pltpu.VMEM((D_MODEL // LANES, d_qkv, LANES), F32),
                        pltpu.VMEM((d_qkv, D_MODEL), BF16),
                        pltpu.VMEM((D_MODEL, D_FOURIER), BF16)],
        compiler_params=_params(48),
        name="proj_even",
    )(*toks, mod, g, w_in, gain, _ROPE_COS_T, _ROPE_SIN_T, *_CC, *_SC)


def _attn_kernel(par_ref, qt_ref, k_ref, vt_ref, o_ref, ot_scr):
    qi = pl.program_id(1)
    is_latent = qi < NQ
    use_bound = par_ref[0] > 0.5
    k_norm_bound = par_ref[1]
    blocks_per_chunk = CH_ONLINE // TM_E

    def group_consts(g):
        pair, side = divmod(g, 2)
        lanes = slice(LANES * pair, LANES * (pair + 1))
        rows = slice(V_ROWS * g, V_ROWS * (g + 1))
        qpads, qnorms = [], []
        for hh in range(GQA_GROUP):
            row0 = (GQA_GROUP * g + hh) * HEAD_DIM
            q_h = qt_ref[row0:row0 + HEAD_DIM, :]
            zero = jnp.zeros_like(q_h)
            qpads.append(jnp.concatenate([q_h, zero] if side == 0 else [zero, q_h], axis=0))
            qf = q_h.astype(F32)
            qnorms.append(jnp.sqrt(jnp.sum(qf * qf, axis=0, keepdims=True)))
        return lanes, rows, qpads, qnorms

    def pv(vt_blocks, p, acc):
        for j, vt in enumerate(vt_blocks):
            acc = acc + jnp.dot(vt, p[TM_E * j:TM_E * (j + 1), :], preferred_element_type=F32)
        return acc

    def store_heads(g, accs):
        for hh, acc in enumerate(accs):
            row0 = (GQA_GROUP * g + hh) * HEAD_DIM
            ot_scr[row0:row0 + HEAD_DIM, :] = acc[0:HEAD_DIM] / acc[HEAD_DIM:HEAD_DIM + 1]

    def online_step(k_chunk, vt_blocks, qpads, carries):
        sts = [jnp.dot(k_chunk, qpad, preferred_element_type=F32) for qpad in qpads]
        mid = []
        for st, (m, acc) in zip(sts, carries):
            m_new = jnp.maximum(m, jnp.max(st, axis=0, keepdims=True))
            mid.append((m_new, jnp.exp2(m - m_new), jnp.exp2(st - m_new).astype(BF16)))
        return tuple((m_new, pv(vt_blocks, p, alpha * acc))
                     for (m_new, alpha, p), (_, acc) in zip(mid, carries))

    def chunk_operands(c, lanes, rows):
        k0 = pl.multiple_of(c * CH_ONLINE, CH_ONLINE)
        return (k_ref[pl.ds(k0, CH_ONLINE), lanes],
                [vt_ref[c * blocks_per_chunk + j, rows, :] for j in range(blocks_per_chunk)])

    zero_acc = jnp.zeros((V_ROWS, TQ), F32)
    online_init = (jnp.full((1, TQ), -jnp.inf, F32), zero_acc)

    @pl.when(jnp.logical_and(is_latent, use_bound))
    def _():
        for g in range(N_KV_HEADS):
            lanes, rows, qpads, qnorms = group_consts(g)
            bounds = [qn * k_norm_bound for qn in qnorms]

            vt_blocks = [vt_ref[j, rows, :] for j in range(KV_BLOCKS)]
            sts = [jnp.dot(k_ref[:, lanes], qpad, preferred_element_type=F32) for qpad in qpads]
            ps = [jnp.exp2(st - u).astype(BF16) for st, u in zip(sts, bounds)]
            store_heads(g, [pv(vt_blocks, p, zero_acc) for p in ps])

    @pl.when(jnp.logical_and(is_latent, jnp.logical_not(use_bound)))
    def _():
        for g in range(N_KV_HEADS):
            lanes, rows, qpads, _ = group_consts(g)

            def body(c, carries, lanes=lanes, rows=rows, qpads=qpads):
                k_chunk, vt_blocks = chunk_operands(c, lanes, rows)
                return online_step(k_chunk, vt_blocks, qpads, carries)

            carries = lax.fori_loop(0, K_ROWS // CH_ONLINE, body, (online_init,) * GQA_GROUP)
            store_heads(g, [acc for _, acc in carries])

    @pl.when(jnp.logical_not(is_latent))
    def _():
        for g in range(N_KV_HEADS):
            lanes, rows, qpads, _ = group_consts(g)
            carries = online_step(k_ref[SEQ:K_ROWS, lanes], [vt_ref[KV_BLOCKS - 1, rows, :]], qpads,
                                  (online_init,) * GQA_GROUP)
            store_heads(g, [acc for _, acc in carries])

    o_ref[...] = ot_scr[...].T.astype(BF16)


def _attn_call(par, qt, k, vt3):
    def q_block(b, qi):
        return jnp.where(qi < NQ, b * NQ + qi, N_X // TQ + b)

    return pl.pallas_call(
        _attn_kernel,
        grid=(BATCH, NQ + 1),
        in_specs=[
            pl.BlockSpec(memory_space=pltpu.SMEM),
            pl.BlockSpec((D_Q, TQ), lambda b, qi: (0, q_block(b, qi))),
            pl.BlockSpec((K_ROWS, D_KV), lambda b, qi: (b, 0), pipeline_mode=pl.Buffered(1)),
            pl.BlockSpec((KV_BLOCKS, N_KV_HEADS * V_ROWS, TM_E), lambda b, qi: (b, 0, 0),
                         pipeline_mode=pl.Buffered(1)),
        ],
        out_specs=pl.BlockSpec((TQ, D_Q), lambda b, qi: (q_block(b, qi), 0)),
        out_shape=jax.ShapeDtypeStruct((N_TOK, D_Q), BF16),
        scratch_shapes=[pltpu.VMEM((D_Q, TQ), F32)],
        compiler_params=_params(58),
        name="attn",
    )(par, qt, k, vt3)


def _attn_params(q_gain, k_gain):
    slack = 1.0 + 2.0 ** -6
    k_norm_bound = (HEAD_DIM ** 0.5) * jnp.max(jnp.abs(k_gain)) * slack
    q_norm_bound = (HEAD_DIM ** 0.5) * jnp.max(jnp.abs(q_gain)) * Q_SCALE * slack
    use_bound = (q_norm_bound * k_norm_bound < MAX_SCORE_BOUND).astype(F32)
    return jnp.stack([use_bound, k_norm_bound]).astype(F32)


def _fourier1_kernel(w_ref, m_ref, z_ref):
    for j in range(TA):
        xa = w_ref[:, j, :]
        y = jnp.dot(m_ref[j], xa, precision=HIGHEST, preferred_element_type=F32)
        z_ref[j, 0:FFT_B, :] = y[0:FFT_B, 0:D_FOURIER] + y[FFT_B:, D_FOURIER:]
        z_ref[j, FFT_B:, :] = y[0:FFT_B, D_FOURIER:] - y[FFT_B:, 0:D_FOURIER]


def _fourier1_call(w):
    w3 = w.reshape(N_TOK // FFT_A, FFT_A, 2 * D_FOURIER)
    return pl.pallas_call(
        _fourier1_kernel,
        grid=(BATCH, FFT_A // TA),
        in_specs=[
            pl.BlockSpec((FFT_B, TA, 2 * D_FOURIER), lambda b, a: (b, a, 0)),
            pl.BlockSpec((TA, 2 * FFT_B, FFT_B), lambda b, a: (a, 0, 0)),
        ],
        out_specs=pl.BlockSpec((None, TA, 2 * FFT_B, D_FOURIER), lambda b, a: (b, a, 0, 0)),
        out_shape=jax.ShapeDtypeStruct((BATCH, FFT_A, 2 * FFT_B, D_FOURIER), F32),
        compiler_params=_params(32),
        name="fourier1",
    )(w3, _STAGE1)


def _fourier2_kernel(zr_ref, zi_ref, cs_ref, o_ref):
    zr = jnp.concatenate([zr_ref[:, j, :] for j in range(TB2)], axis=1)
    zi = jnp.concatenate([zi_ref[:, j, :] for j in range(TB2)], axis=1)
    (rh, rl), (ih, il) = _split(zr), _split(zi)
    rhs = jnp.concatenate([rh, rh, rl, ih, ih, il], axis=0)
    res = jnp.dot(cs_ref[...], rhs, preferred_element_type=F32) * FOURIER_SCALE_X
    for j in range(TB2):
        o_ref[:, j, :] = res[:, D_FOURIER * j:D_FOURIER * (j + 1)]


def _fourier2_call(z):
    n_j = FFT_B // TB2
    out = pl.pallas_call(
        _fourier2_kernel,
        grid=(BATCH, n_j),
        in_specs=[
            pl.BlockSpec((None, FFT_A, TB2, D_FOURIER), lambda b, j: (b, 0, j, 0)),
            pl.BlockSpec((None, FFT_A, TB2, D_FOURIER), lambda b, j: (b, 0, j + n_j, 0)),
            _const_spec((FFT_A, 6 * FFT_A)),
        ],
        out_specs=pl.BlockSpec((None, FFT_A, TB2, D_FOURIER), lambda b, j: (b, 0, j, 0)),
        out_shape=jax.ShapeDtypeStruct((BATCH, FFT_A, FFT_B, D_FOURIER), F32),
        compiler_params=_params(32),
        name="fourier2",
    )(z, z, _CS128)
    return out.reshape(N_X, D_FOURIER)


def _fourier_ctx_kernel(w_ref, ch_ref, cl_ref, sh_ref, sl_ref, o_ref):
    w = w_ref[...]
    o_ref[...] = (_dot3((ch_ref[...], cl_ref[...]), _split(w[:, 0:D_FOURIER]))
                  + _dot3((sh_ref[...], sl_ref[...]), _split(w[:, D_FOURIER:]))) * FOURIER_SCALE_C


def _fourier_ctx_call(w):
    return pl.pallas_call(
        _fourier_ctx_kernel,
        grid=(BATCH,),
        in_specs=[
            pl.BlockSpec((CTX_LEN, 2 * D_FOURIER), lambda b: (N_X // CTX_LEN + b, 0)),
        ] + [_const_spec((CTX_LEN, CTX_LEN))] * 4,
        out_specs=pl.BlockSpec((CTX_LEN, D_FOURIER), lambda b: (b, 0)),
        out_shape=jax.ShapeDtypeStruct((N_C, D_FOURIER), F32),
        name="fourier_ctx",
    )(w, *_C256, *_S256)


def _proj_odd_kernel(x_ref, mod_ref, g_ref, w_ref, b_ref, u_ref):
    h = _norm_mod(x_ref[...], g_ref[...], mod_ref[0:1, :], mod_ref[1:2, :]).astype(BF16)
    a = jnp.dot(h, w_ref[...], preferred_element_type=F32) + b_ref[...]
    u_ref[...] = a[:, 0:D_MODEL] * jax.nn.sigmoid(a[:, D_MODEL:])


def _proj_odd_call(tok, mod, g, w, j, b, n_blocks):
    return pl.pallas_call(
        _proj_odd_kernel,
        grid=(n_blocks,),
        in_specs=[
            pl.BlockSpec((TM, D_MODEL), lambda i: (i, 0)),
            pl.BlockSpec((None, N_MOD, D_MODEL), lambda i: (_mod_block(i), 0, 0)),
            _const_spec((1, D_MODEL)),
            _layer_resident(w, j),
            _const_spec((1, 2 * D_MODEL)),
        ],
        out_specs=pl.BlockSpec((TM, D_MODEL), lambda i: (i, 0)),
        out_shape=jax.ShapeDtypeStruct((n_blocks * TM, D_MODEL), F32),
        compiler_params=_params(48),
        name="proj_odd",
    )(tok, mod, g, w, b)


def _dwconv_kernel(prev_ref, cur_ref, next_ref, w_ref, b_ref, lng_ref, lnb_ref, o_ref,
                   ext_scr, conv_scr):
    i = pl.program_id(0)
    blocks_per_seq = SEQ // CONV_TM
    is_ctx = i >= N_X // CONV_TM
    first = jnp.logical_or(is_ctx, i % blocks_per_seq == 0)
    last = jnp.logical_or(is_ctx, i % blocks_per_seq == blocks_per_seq - 1)
    ext_scr[0:CONV_HALO, :] = jnp.where(first, 0.0, prev_ref[...])
    ext_scr[CONV_HALO:CONV_HALO + CONV_TM, :] = cur_ref[...]
    ext_scr[CONV_HALO + CONV_TM:, :] = jnp.where(last, 0.0, next_ref[...])
    off = CONV_HALO - PAD
    win = CONV_RC + SUBLANES
    n_m = (CONV_WIDTH - 1 + off) // SUBLANES + 1
    for lc in range(D_MODEL // LANES):
        lanes = slice(LANES * lc, LANES * (lc + 1))
        for base in range(0, CONV_TM, CONV_RC):
            acc = jnp.broadcast_to(b_ref[:, lanes], (CONV_RC, LANES))
            for r in range(SUBLANES):
                q = None
                for m in range(n_m):
                    k = SUBLANES * m + r - off
                    if 0 <= k < CONV_WIDTH:
                        lo = base + SUBLANES * m
                        term = w_ref[k:k + 1, lanes] * ext_scr[lo:lo + win, lanes]
                        q = term if q is None else q + term
                acc = acc + q[r:r + CONV_RC]
            conv_scr[base:base + CONV_RC, lanes] = acc
    u = conv_scr[...]
    mu = jnp.mean(u, axis=-1, keepdims=True)
    d = u - mu
    var = jnp.mean(d * d, axis=-1, keepdims=True)
    y = d * lax.rsqrt(var + EPS) * lng_ref[...] + lnb_ref[...]
    o_ref[...] = (y * jax.nn.sigmoid(y)).astype(BF16)


def _dwconv_call(u, w_dw, b_dw, ln_g, ln_b):
    n_rows = u.shape[0]
    n_blocks = n_rows // CONV_TM
    halo_per_block = CONV_TM // CONV_HALO
    n_halo = n_rows // CONV_HALO
    return pl.pallas_call(
        _dwconv_kernel,
        grid=(n_blocks,),
        in_specs=[
            pl.BlockSpec((CONV_HALO, D_MODEL), lambda i: (jnp.maximum(i * halo_per_block - 1, 0), 0)),
            pl.BlockSpec((CONV_TM, D_MODEL), lambda i: (i, 0)),
            pl.BlockSpec((CONV_HALO, D_MODEL),
                         lambda i: (jnp.minimum((i + 1) * halo_per_block, n_halo - 1), 0)),
            _const_spec((CONV_WIDTH, D_MODEL)),
            _const_spec((1, D_MODEL)),
            _const_spec((1, D_MODEL)),
            _const_spec((1, D_MODEL)),
        ],
        out_specs=pl.BlockSpec((CONV_TM, D_MODEL), lambda i: (i, 0)),
        out_shape=jax.ShapeDtypeStruct((n_rows, D_MODEL), BF16),
        scratch_shapes=[pltpu.VMEM((CONV_TM + 2 * CONV_HALO, D_MODEL), F32),
                        pltpu.VMEM((CONV_TM, D_MODEL), F32)],
        compiler_params=_params(32),
        name="dwconv",
    )(u, u, u, w_dw, b_dw, ln_g, ln_b)


def _mlp_tail(x1, mod_ref, g2_ref, w1_ref, w2_ref, out_ref):
    h2 = _norm_mod(x1, g2_ref[...], mod_ref[3:4, :], mod_ref[4:5, :]).astype(BF16)
    acc = jnp.zeros((TM, D_MODEL), F32)
    for ck in range(D_FF // FF_CHUNK):
        cols = slice(FF_CHUNK * ck, FF_CHUNK * (ck + 1))
        a = jnp.maximum(jnp.dot(h2, w1_ref[:, cols].astype(BF16), preferred_element_type=F32), 0.0)
        acc = acc + jnp.dot((a * a).astype(BF16), w2_ref[cols, :].astype(BF16),
                            preferred_element_type=F32)
    out_ref[...] = x1 + mod_ref[5:6, :] * acc


def _post_even_kernel(*refs, n_tok_refs):
    (mod_ref, o_ref, fx_ref, fc_ref, wo_ref, g2_ref, w1_ref, w2_ref, out_ref) = refs[n_tok_refs:]
    i = pl.program_id(0)
    fo = jnp.where(i >= X_BLOCKS, fc_ref[...], fx_ref[...]).astype(BF16)
    y = (jnp.dot(o_ref[...], wo_ref[0:D_Q, :], preferred_element_type=F32)
         + jnp.dot(fo, wo_ref[D_Q:, :], preferred_element_type=F32))
    x1 = _token_block(refs, n_tok_refs, X_BLOCKS) + mod_ref[2:3, :] * y
    _mlp_tail(x1, mod_ref, g2_ref, w1_ref, w2_ref, out_ref)


def _post_odd_kernel(x_ref, mod_ref, v_ref, wo_ref, bo_ref, g2_ref, w1_ref, w2_ref, out_ref):
    y = jnp.dot(v_ref[...], wo_ref[...], preferred_element_type=F32) + bo_ref[...]
    x1 = x_ref[...] + mod_ref[2:3, :] * y
    _mlp_tail(x1, mod_ref, g2_ref, w1_ref, w2_ref, out_ref)


def _layer_resident(stack, layer):
    shape = stack.shape[1:]
    return pl.BlockSpec((None,) + shape, lambda *_: (layer,) + (0,) * len(shape),
                        pipeline_mode=pl.Buffered(1))


def _post_even_call(tok, mod, o, f_x, f_c, wo, j, g2, w1, w2, layer):
    toks, tok_specs = _token_specs(tok, TM)
    return pl.pallas_call(
        functools.partial(_post_even_kernel, n_tok_refs=len(toks)),
        grid=(TOK_BLOCKS,),
        in_specs=tok_specs + [
            pl.BlockSpec((None, N_MOD, D_MODEL), lambda i: (_mod_block(i), 0, 0)),
            pl.BlockSpec((TM, D_Q), lambda i: (i, 0)),
            pl.BlockSpec((TM, D_FOURIER), lambda i: (jnp.minimum(i, X_BLOCKS - 1), 0)),
            _const_spec((N_C, D_FOURIER)),
            _layer_resident(wo, j),
            _const_spec((1, D_MODEL)),
            _layer_resident(w1, layer),
            _layer_resident(w2, layer),
        ],
        out_specs=pl.BlockSpec((TM, D_MODEL), lambda i: (i, 0)),
        out_shape=jax.ShapeDtypeStruct((N_TOK, D_MODEL), F32),
        compiler_params=_params(60),
        name="post_even",
    )(*toks, mod, o, f_x, f_c, wo, g2, w1, w2)


def _post_odd_call(tok, mod, v, wo, j, bo, g2, w1, w2, layer, n_blocks):
    return pl.pallas_call(
        _post_odd_kernel,
        grid=(n_blocks,),
        in_specs=[
            pl.BlockSpec((TM, D_MODEL), lambda i: (i, 0)),
            pl.BlockSpec((None, N_MOD, D_MODEL), lambda i: (_mod_block(i), 0, 0)),
            pl.BlockSpec((TM, D_MODEL), lambda i: (i, 0)),
            _layer_resident(wo, j),
            _const_spec((1, D_MODEL)),
            _const_spec((1, D_MODEL)),
            _layer_resident(w1, layer),
            _layer_resident(w2, layer),
        ],
        out_specs=pl.BlockSpec((TM, D_MODEL), lambda i: (i, 0)),
        out_shape=jax.ShapeDtypeStruct((n_blocks * TM, D_MODEL), F32),
        compiler_params=_params(60),
        name="post_odd",
    )(tok, mod, v, wo, bo, g2, w1, w2)


def _split_halves(w_cols):
    lead = w_cols.shape[:-1]
    n_heads = w_cols.shape[-1] // HEAD_DIM
    w4 = w_cols.reshape(*lead, n_heads, HALF, 2)
    return jnp.swapaxes(w4, -1, -2).reshape(*lead, n_heads * HEAD_DIM)


def kernel(x, c, ctx, c_ctx, ada_w, ada_b, norm1_g, norm2_g, mlp_w1, mlp_w2, attn_w_in, q_norm_g, k_norm_g, attn_w_out, conv_w_pw1, conv_b_pw1, conv_w_dw, conv_b_dw, conv_ln_g, conv_ln_b, conv_w_pw2, conv_b_pw2):
    tok = (x.reshape(N_X, D_MODEL), ctx.reshape(N_C, D_MODEL))
    c8 = jnp.concatenate([c, c_ctx[None, :], jnp.zeros((SUBLANES - BATCH - 1, D_MODEL), F32)], axis=0)
    mods = _ada_call(c8, ada_w, ada_b)[:, :BATCH + 1].reshape(DEPTH, BATCH + 1, N_MOD, D_MODEL)

    w1 = mlp_w1
    w2 = mlp_w2
    w_out = attn_w_out.astype(BF16)
    w_pw1 = conv_w_pw1.astype(BF16)
    w_pw2 = conv_w_pw2.astype(BF16)

    for i in range(DEPTH):
        j = i // 2
        last = i == DEPTH - 1
        n_blocks = X_BLOCKS if last else TOK_BLOCKS
        mod = mods[i]
        g1 = norm1_g[i][None, :]
        g2 = norm2_g[i][None, :]
        if i % 2 == 0:
            gain = jnp.concatenate([jnp.tile(_split_halves(q_norm_g[j]), N_Q_HEADS),
                                    jnp.tile(_split_halves(k_norm_g[j]), N_KV_HEADS)])[:, None]
            qt, k, vt3, w = _proj_even_call(tok, mod, g1, attn_w_in, j, gain)
            o = _attn_call(_attn_params(q_norm_g[j], k_norm_g[j]), qt, k, vt3)
            f_x = _fourier2_call(_fourier1_call(w))
            f_c = _fourier_ctx_call(w)
            tok = _post_even_call(tok, mod, o, f_x, f_c, w_out, j, g2, w1, w2, i)
        else:
            u = _proj_odd_call(tok, mod, g1, w_pw1, j, conv_b_pw1[j][None, :], n_blocks)
            v = _dwconv_call(u, conv_w_dw[j], conv_b_dw[j][None, :], conv_ln_g[j][None, :],
                             conv_ln_b[j][None, :])
            tok = _post_odd_call(tok, mod, v, w_pw2, j, conv_b_pw2[j][None, :], g2, w1, w2, i,
                                 n_blocks)
    return tok[:N_X].reshape(BATCH, SEQ, D_MODEL)
```
